```python
import math
import jax, jax.numpy as jnp
from jax import lax
import numpy as np

D_MODEL = 1024
BATCH = 4
SEQ = 8192
DEPTH = 4

HEAD_DIM = 64
SWA_HEADS = 8
SWA_KV_HEADS = 2
SWA_WINDOW = 128
SWA_BLOCK = 128
RNN_WIDTH = D_MODEL
RNN_BLOCKS = 16
RNN_BLOCK_WIDTH = RNN_WIDTH // RNN_BLOCKS
RNN_CONV = 4
RGLRU_C = 8.0
MOBA_HEADS = 8
MOBA_BLOCK = 256
MOBA_TOPK = 3
NUM_BUCKETS = 32
MAX_DISTANCE = 2048
D_FF = 2816
FFN_CONV = 3

RMS_EPS = 1e-6
NEG_INF = -1e30
N_BRANCH = 3
SWA_Q = SWA_HEADS * HEAD_DIM
SWA_KV = SWA_KV_HEADS * HEAD_DIM
MOBA_W = MOBA_HEADS * HEAD_DIM
IN_WIDTHS = (SWA_Q, SWA_KV, SWA_KV, RNN_WIDTH, RNN_WIDTH, MOBA_W, MOBA_W, MOBA_W, N_BRANCH * D_MODEL)
IN_OFFSETS = tuple(sum(IN_WIDTHS[:i + 1]) for i in range(len(IN_WIDTHS) - 1))
D_IN = sum(IN_WIDTHS)
D_MIX = SWA_Q + RNN_WIDTH + MOBA_W

kernel_name = "hybrid_swa_rglru_moba_convffn_trunk"


def rms_norm(x, gain):
    xf = x.astype(jnp.float32)
    y = xf * lax.rsqrt(jnp.mean(xf * xf, axis=-1, keepdims=True) + RMS_EPS)
    return (y * gain.astype(jnp.float32)).astype(x.dtype)


def t5_bucket(dist):
    dist = jnp.maximum(dist, 0)
    max_exact = NUM_BUCKETS // 2
    log_ratio = jnp.log(jnp.maximum(dist, 1).astype(jnp.float32) / max_exact) / math.log(MAX_DISTANCE / max_exact)
    large = max_exact + (log_ratio * (NUM_BUCKETS - max_exact)).astype(jnp.int32)
    large = jnp.minimum(large, NUM_BUCKETS - 1)
    return jnp.where(dist < max_exact, dist, large)


def causal_dwconv(x, w, b):
    k_width = w.shape[0]
    s = x.shape[1]
    xp = jnp.pad(x, ((0, 0), (k_width - 1, 0), (0, 0)))
    return sum(xp[:, k:k + s] * w[k] for k in range(k_width)) + b


def swa_attention(q, k, v, sinks, bias_a):
    b_sz, s_len = q.shape[:2]
    L = SWA_BLOCK
    nblk = s_len // L
    grp = SWA_HEADS // SWA_KV_HEADS
    qb = q.reshape(b_sz, nblk, L, SWA_KV_HEADS, grp, HEAD_DIM)

    def band(t):
        tp = jnp.pad(t, ((0, 0), (L, 0), (0, 0), (0, 0))).reshape(b_sz, nblk + 1, L, SWA_KV_HEADS, HEAD_DIM)
        return jnp.concatenate([tp[:, :-1], tp[:, 1:]], axis=2)

    kb, vb = band(k), band(v)
    s = jnp.einsum('bnqkgd,bnskd->bnkgqs', qb, kb).astype(jnp.float32) * (HEAD_DIM ** -0.5)
    qi = jnp.arange(L)
    sj = jnp.arange(2 * L)
    diff = qi[:, None] + L - sj[None, :]
    bias = bias_a[t5_bucket(diff)].astype(jnp.float32).transpose(2, 0, 1).reshape(SWA_KV_HEADS, grp, L, 2 * L)
    key_pos = jnp.arange(nblk)[:, None] * L - L + sj[None, :]
    valid = ((diff >= 0) & (diff < SWA_WINDOW))[None] & (key_pos >= 0)[:, None, :]
    s = jnp.where(valid[None, :, None, None], s + bias, NEG_INF)
    sink = sinks.astype(jnp.float32).reshape(SWA_KV_HEADS, grp)[:, :, None, None]
    m = jnp.maximum(jnp.max(s, axis=-1, keepdims=True), sink)
    p = jnp.exp(s - m)
    p = p / (jnp.sum(p, axis=-1, keepdims=True) + jnp.exp(sink - m))
    o = jnp.einsum('bnkgqs,bnskd->bnqkgd', p.astype(v.dtype), vb)
    return o.reshape(b_sz, s_len, SWA_Q)


def rg_lru(x, ga_w, ga_b, gx_w, gx_b, lam):
    b_sz, s_len, width = x.shape
    xb = x.reshape(b_sz, s_len, RNN_BLOCKS, RNN_BLOCK_WIDTH)
    r = jax.nn.sigmoid(jnp.einsum('bsnc,ncd->bsnd', xb, ga_w).reshape(b_sz, s_len, width) + ga_b)
    i = jax.nn.sigmoid(jnp.einsum('bsnc,ncd->bsnd', xb, gx_w).reshape(b_sz, s_len, width) + gx_b)
    log_a = -RGLRU_C * r.astype(jnp.float32) * jax.nn.softplus(-lam.astype(jnp.float32))
    a = jnp.exp(log_a)
    mult = jnp.sqrt(-jnp.expm1(2.0 * log_a))
    mult = jnp.where((jnp.arange(s_len) == 0)[None, :, None], 1.0, mult)
    u = mult * (i * x).astype(jnp.float32)

    def combine(left, right):
        a_l, b_l = left
        a_r, b_r = right
        return (a_l * a_r, a_r * b_l + b_r)

    _, h = lax.associative_scan(combine, (a, u), axis=1)
    return h.astype(x.dtype)


def moba_attention(q, k, v, bias_c):
    b_sz, s_len = q.shape[:2]
    MB = MOBA_BLOCK
    nb = -(-s_len // MB)
    s_pad = nb * MB
    pad = ((0, 0), (0, s_pad - s_len), (0, 0), (0, 0))
    q, k, v = jnp.pad(q, pad), jnp.pad(k, pad), jnp.pad(v, pad)
    topk = min(MOBA_TOPK, nb)
    kb = k.reshape(b_sz, nb, MB, MOBA_HEADS, HEAD_DIM)
    vb = v.reshape(b_sz, nb, MB, MOBA_HEADS, HEAD_DIM)
    k_mean = jnp.mean(kb.astype(jnp.float32), axis=2).astype(k.dtype)
    k_bh = kb.transpose(0, 3, 1, 2, 4)
    v_bh = vb.transpose(0, 3, 1, 2, 4)
    q_chunks = q.reshape(b_sz, nb, MB, MOBA_HEADS, HEAD_DIM).transpose(1, 0, 2, 3, 4)
    k_own = kb.transpose(1, 0, 2, 3, 4)
    v_own = vb.transpose(1, 0, 2, 3, 4)
    pos = jnp.arange(MB)
    own_diff = pos[:, None] - pos[None, :]
    own_mask = own_diff >= 0
    own_bias = bias_c[t5_bucket(own_diff)].astype(jnp.float32).transpose(0, 2, 1)
    b_ix = jnp.arange(b_sz)[:, None, None, None]
    h_ix = jnp.arange(MOBA_HEADS)[None, None, :, None]
    blk_ids = jnp.arange(nb)
    scale = HEAD_DIM ** -0.5

    def one_block(args):
        n, q_c, k_c, v_c = args
        gate = jnp.einsum('bqhd,bmhd->bqhm', q_c, k_mean).astype(jnp.float32)
        gate = jnp.where(blk_ids < n, gate, NEG_INF)
        _, sel = lax.top_k(gate, topk)
        sel_valid = sel < n
        k_sel = k_bh[b_ix, h_ix, sel]
        v_sel = v_bh[b_ix, h_ix, sel]
        s_sel = jnp.einsum('bqhd,bqhtsd->bqhts', q_c, k_sel).astype(jnp.float32) * scale
        dist = (n * MB + pos)[None, :, None, None, None] - (sel[..., None] * MB + pos)
        s_sel = s_sel + bias_c[t5_bucket(dist), h_ix[..., None]].astype(jnp.float32)
        s_sel = jnp.where(sel_valid[..., None], s_sel, NEG_INF).reshape(b_sz, MB, MOBA_HEADS, topk * MB)
        s_own = jnp.einsum('bqhd,bshd->bqhs', q_c, k_c).astype(jnp.float32) * scale + own_bias
        s_own = jnp.where(own_mask[:, None, :], s_own, NEG_INF)
        p = jax.nn.softmax(jnp.concatenate([s_sel, s_own], axis=-1), axis=-1).astype(v.dtype)
        p_sel = p[..., :topk * MB].reshape(b_sz, MB, MOBA_HEADS, topk, MB)
        p_own = p[..., topk * MB:]
        return jnp.einsum('bqhts,bqhtsd->bqhd', p_sel, v_sel) + jnp.einsum('bqhs,bshd->bqhd', p_own, v_c)

    o = lax.map(one_block, (blk_ids, q_chunks, k_own, v_own))
    return o.transpose(1, 0, 2, 3, 4).reshape(b_sz, s_pad, MOBA_W)[:, :s_len]


def setup_inputs(seed: int = 0) -> dict:
    key = jax.random.key(seed)
    ks = jax.random.split(key, 32)
    f32 = jnp.float32

    def nrm(k, shape, fan_in, scale=1.0):
        return jax.random.normal(k, shape, f32) * (scale * fan_in ** -0.5)

    def gain(k, shape):
        return 1.0 + 0.05 * jax.random.normal(k, shape, f32)

    def small(k, shape, scale=0.02):
        return scale * jax.random.normal(k, shape, f32)

    a_init = jax.random.uniform(ks[16], (DEPTH, RNN_WIDTH), f32, 0.9, 0.999)
    w_branch = jnp.concatenate([
        nrm(ks[20], (DEPTH, SWA_Q, D_MODEL), SWA_Q),
        nrm(ks[21], (DEPTH, RNN_WIDTH, D_MODEL), RNN_WIDTH),
        nrm(ks[22], (DEPTH, MOBA_W, D_MODEL), MOBA_W)], axis=1)
    return {
        'x': jax.random.normal(ks[0], (BATCH, SEQ, D_MODEL), f32),
        'c': jax.random.normal(ks[1], (BATCH, D_MODEL), f32),
        'w_mod': nrm(ks[2], (DEPTH, D_MODEL, 6 * D_MODEL), D_MODEL, 0.5),
        'b_mod': small(ks[3], (DEPTH, 6 * D_MODEL)),
        'norm_mix': gain(ks[4], (DEPTH, D_MODEL)),
        'norm_ffn': gain(ks[5], (DEPTH, D_MODEL)),
        'w_in': nrm(ks[6], (DEPTH, D_MODEL, D_IN), D_MODEL),
        'qnorm_a': gain(ks[7], (DEPTH, HEAD_DIM)),
        'knorm_a': gain(ks[8], (DEPTH, HEAD_DIM)),
        'sinks': small(ks[9], (DEPTH, SWA_HEADS), 0.5),
        'rnn_conv_w': nrm(ks[10], (DEPTH, RNN_CONV, RNN_WIDTH), RNN_CONV),
        'rnn_conv_b': small(ks[11], (DEPTH, RNN_WIDTH)),
        'rnn_gate_a_w': nrm(ks[12], (DEPTH, RNN_BLOCKS, RNN_BLOCK_WIDTH, RNN_BLOCK_WIDTH), RNN_BLOCK_WIDTH),
        'rnn_gate_a_b': small(ks[13], (DEPTH, RNN_WIDTH)),
        'rnn_gate_x_w': nrm(ks[14], (DEPTH, RNN_BLOCKS, RNN_BLOCK_WIDTH, RNN_BLOCK_WIDTH), RNN_BLOCK_WIDTH),
        'rnn_gate_x_b': small(ks[15], (DEPTH, RNN_WIDTH)),
        'rnn_lambda': jnp.log(a_init) - jnp.log1p(-a_init),
        'qnorm_c': gain(ks[17], (DEPTH, HEAD_DIM)),
        'knorm_c': gain(ks[18], (DEPTH, HEAD_DIM)),
        'rel_bias': small(ks[19], (NUM_BUCKETS, SWA_HEADS + MOBA_HEADS), 0.5),
        'w_branch': w_branch,
        'w_out': nrm(ks[23], (DEPTH, D_MODEL, D_MODEL), D_MODEL),
        'w_up': nrm(ks[24], (DEPTH, D_MODEL, 2 * D_FF), D_MODEL),
        'ffn_conv_w': nrm(ks[25], (DEPTH, FFN_CONV, 2 * D_FF), FFN_CONV),
        'ffn_conv_b': small(ks[26], (DEPTH, 2 * D_FF)),
        'w_down': nrm(ks[27], (DEPTH, D_FF, D_MODEL), D_FF),
    }


def reference(x, c, w_mod, b_mod, norm_mix, norm_ffn, w_in, qnorm_a, knorm_a, sinks,
              rnn_conv_w, rnn_conv_b, rnn_gate_a_w, rnn_gate_a_b, rnn_gate_x_w, rnn_gate_x_b,
              rnn_lambda, qnorm_c, knorm_c, rel_bias, w_branch, w_out, w_up, ffn_conv_w,
              ffn_conv_b, w_down):
    b_sz, s_len, _ = x.shape
    c_act = jax.nn.silu(c)
    bias_a = rel_bias[:, :SWA_HEADS]
    bias_c = rel_bias[:, SWA_HEADS:]
    for l in range(DEPTH):
        mod = (c_act @ w_mod[l] + b_mod[l])[:, None, :]
        shift_m, scale_m, gate_m, shift_f, scale_f, gate_f = jnp.split(mod, 6, axis=-1)

        h = rms_norm(x, norm_mix[l]) * (1.0 + scale_m) + shift_m
        z = h @ w_in[l]
        qa, ka, va, xr, yr, qc, kc, vc, g_logits = jnp.split(z, IN_OFFSETS, axis=-1)

        qa = rms_norm(qa.reshape(b_sz, s_len, SWA_HEADS, HEAD_DIM), qnorm_a[l])
        ka = rms_norm(ka.reshape(b_sz, s_len, SWA_KV_HEADS, HEAD_DIM), knorm_a[l])
        va = va.reshape(b_sz, s_len, SWA_KV_HEADS, HEAD_DIM)
        o_a = swa_attention(qa, ka, va, sinks[l], bias_a)

        xr = causal_dwconv(xr, rnn_conv_w[l], rnn_conv_b[l])
        hr = rg_lru(xr, rnn_gate_a_w[l], rnn_gate_a_b[l], rnn_gate_x_w[l], rnn_gate_x_b[l], rnn_lambda[l])
        o_b = hr * jax.nn.gelu(yr)

        qc = rms_norm(qc.reshape(b_sz, s_len, MOBA_HEADS, HEAD_DIM), qnorm_c[l])
        kc = rms_norm(kc.reshape(b_sz, s_len, MOBA_HEADS, HEAD_DIM), knorm_c[l])
        vc = vc.reshape(b_sz, s_len, MOBA_HEADS, HEAD_DIM)
        o_c = moba_attention(qc, kc, vc, bias_c)

        wb = w_branch[l]
        p_a = o_a @ wb[:SWA_Q]
        p_b = o_b @ wb[SWA_Q:SWA_Q + RNN_WIDTH]
        p_c = o_c @ wb[SWA_Q + RNN_WIDTH:]
        g = jax.nn.sigmoid(g_logits.reshape(b_sz, s_len, N_BRANCH, D_MODEL))
        merged = g[:, :, 0] * p_a + g[:, :, 1] * p_b + g[:, :, 2] * p_c
        x = x + gate_m * (merged @ w_out[l])

        h = rms_norm(x, norm_ffn[l]) * (1.0 + scale_f) + shift_f
        u = causal_dwconv(h @ w_up[l], ffn_conv_w[l], ffn_conv_b[l])
        u_gate, u_val = jnp.split(u, 2, axis=-1)
        x = x + gate_f * ((jax.nn.silu(u_gate) * u_val) @ w_down[l])
    return x
```

```python
import functools
import math

import numpy as np
import jax
import jax.numpy as jnp
from jax import lax
from jax.experimental import pallas as pl
from jax.experimental.pallas import tpu as pltpu

D_MODEL = 1024
DEPTH = 4
HEAD_DIM = 64
SWA_HEADS = 8
SWA_KV_HEADS = 2
SWA_WINDOW = 128
SWA_BLOCK = 128
RNN_WIDTH = D_MODEL
RNN_BLOCKS = 16
RNN_BLOCK_WIDTH = RNN_WIDTH // RNN_BLOCKS
RNN_CONV = 4
RGLRU_C = 8.0
MOBA_HEADS = 8
MOBA_BLOCK = 256
MOBA_TOPK = 3
NUM_BUCKETS = 32
MAX_DISTANCE = 2048
D_FF = 2816
FFN_CONV = 3
RMS_EPS = 1e-6
NEG_INF = -1e30
TAKEN = -3e38
SWA_Q = SWA_HEADS * HEAD_DIM
SWA_KV = SWA_KV_HEADS * HEAD_DIM
MOBA_W = MOBA_HEADS * HEAD_DIM
SWA_GROUP = SWA_HEADS // SWA_KV_HEADS

LANES = 128
SUBLANES = 8
VMEM_LIMIT_BYTES = 56 * 1024 * 1024

MXU_DTYPE = jnp.bfloat16
SM_SCALE = HEAD_DIM ** -0.5

TM_IN = 512
TM_MERGE = 512
TM_FFN = 512
TN_FFN = 256
T_RNN = 256
C_RNN = 256


def _params(semantics):
    return pltpu.CompilerParams(dimension_semantics=semantics, vmem_limit_bytes=VMEM_LIMIT_BYTES)


def _dot(a, b):
    return jnp.dot(a, b, preferred_element_type=jnp.float32)


def _dot_nt(a, b):
    return lax.dot_general(a, b, (((1,), (1,)), ((), ())), preferred_element_type=jnp.float32)


def _sigmoid(x):
    return 1.0 / (1.0 + jnp.exp(-x))


def _t5_bucket_np(dist):
    dist = np.maximum(dist, 0)
    max_exact = NUM_BUCKETS // 2
    ratio = np.maximum(dist, 1).astype(np.float32) / np.float32(max_exact)
    log_ratio = np.log(ratio).astype(np.float32) / np.float32(math.log(MAX_DISTANCE / max_exact))
    large = max_exact + (log_ratio * np.float32(NUM_BUCKETS - max_exact)).astype(np.int32)
    large = np.minimum(large, NUM_BUCKETS - 1)
    return np.where(dist < max_exact, dist, large).astype(np.int32)


def _moba_near_blocks():
    delta = 0
    while True:
        lo = max(delta * MOBA_BLOCK - (MOBA_BLOCK - 1), 0)
        if int(_t5_bucket_np(np.array([lo]))[0]) == NUM_BUCKETS - 1:
            return delta
        delta += 1


MOBA_NEAR = _moba_near_blocks()


def _mod_kernel(c_ref, w_ref, b_ref, o_ref):
    c = c_ref[...]
    c_act = (c * _sigmoid(c)).astype(MXU_DTYPE)
    o_ref[0] = _dot(c_act, w_ref[0].astype(MXU_DTYPE)) + b_ref[0]


def _mod_call(c_pad, w_mod, b_mod):
    depth, d, d6 = w_mod.shape
    tn = 1536
    return pl.pallas_call(
        _mod_kernel,
        grid=(depth, d6 // tn),
        in_specs=[
            pl.BlockSpec((SUBLANES, d), lambda l, j: (0, 0)),
            pl.BlockSpec((1, d, tn), lambda l, j: (l, 0, j)),
            pl.BlockSpec((1, 1, tn), lambda l, j: (l, 0, j)),
        ],
        out_specs=pl.BlockSpec((1, SUBLANES, tn), lambda l, j: (l, 0, j)),
        out_shape=jax.ShapeDtypeStruct((depth, SUBLANES, d6), jnp.float32),
        compiler_params=_params(("arbitrary", "arbitrary")),
        name="adaln_mod",
    )(c_pad, w_mod, b_mod.reshape(depth, 1, d6))


_C_QA = 0
_C_KVA = _C_QA + SWA_Q
_C_XR = _C_KVA + 2 * SWA_KV
_C_YR = _C_XR + RNN_WIDTH
_C_KC = _C_YR + RNN_WIDTH
_C_G = _C_KC + MOBA_W
_C_END = _C_G + 3 * D_MODEL


def _head_norm_rows(z, ones_blk, gain):
    zz = z * z
    hi = zz.astype(jnp.bfloat16)
    lo = (zz - hi.astype(jnp.float32)).astype(jnp.bfloat16)
    outs = []
    for c in range(z.shape[1] // LANES):
        sl = slice(c * LANES, (c + 1) * LANES)
        ss = _dot(hi[:, sl], ones_blk) + _dot(lo[:, sl], ones_blk)
        outs.append(z[:, sl] * lax.rsqrt(ss * (1.0 / HEAD_DIM) + RMS_EPS))
    y = outs[0] if len(outs) == 1 else jnp.concatenate(outs, axis=1)
    return y * gain


def _head_norm_cols(zt, gain_col):
    w, m = zt.shape
    z3 = zt.reshape(w // HEAD_DIM, HEAD_DIM, m)
    ss = jnp.sum(z3 * z3, axis=1, keepdims=True)
    y = z3 * lax.rsqrt(ss * (1.0 / HEAD_DIM) + RMS_EPS)
    return y.reshape(w, m) * gain_col


def _inproj_kernel(x_ref, mod_ref, gain_ref, wn_ref, wt_ref, ones_ref, gq_a_ref, gk_a_ref, gk_c_ref,
                   gq_c_ref, qa_ref, ka_ref, va_ref, xr_ref, yr_ref, kc_ref, kmean_ref, g_ref,
                   qct_ref, vct_ref):
    d = D_MODEL
    x = x_ref[...]
    shift = mod_ref[0, :, 0:d]
    scale = mod_ref[0, :, d:2 * d]
    ms = jnp.mean(x * x, axis=-1, keepdims=True)
    y = x * lax.rsqrt(ms + RMS_EPS) * gain_ref[...]
    h = (y * (1.0 + scale) + shift).astype(MXU_DTYPE)
    ones_blk = ones_ref[...]

    z = _dot(h, wn_ref[:, _C_QA:_C_KVA])
    qa_ref[...] = (_head_norm_rows(z, ones_blk, gq_a_ref[...]) * SM_SCALE).astype(qa_ref.dtype)

    z = _dot(h, wn_ref[:, _C_KVA:_C_XR])
    ka_ref[...] = _head_norm_rows(z[:, :SWA_KV], ones_blk, gk_a_ref[...]).astype(ka_ref.dtype)
    va_ref[...] = z[:, SWA_KV:].astype(va_ref.dtype)

    xr_ref[...] = _dot(h, wn_ref[:, _C_XR:_C_YR])
    yr_ref[...] = _dot(h, wn_ref[:, _C_YR:_C_KC])

    z = _dot(h, wn_ref[:, _C_KC:_C_G])
    kc = _head_norm_rows(z, ones_blk, gk_c_ref[...])
    kc_ref[...] = kc.astype(kc_ref.dtype)
    tm = x.shape[0]
    nblk = tm // MOBA_BLOCK
    kmean_ref[0] = jnp.mean(kc.reshape(nblk, MOBA_BLOCK, MOBA_W), axis=1)

    for c in range(3 * D_MODEL // 512):
        g_ref[:, c * 512:(c + 1) * 512] = _dot(h, wn_ref[:, _C_G + c * 512:_C_G + (c + 1) * 512])

    zt = _dot_nt(wt_ref[0:MOBA_W, :], h)
    qct = (_head_norm_cols(zt, gq_c_ref[...]) * SM_SCALE).astype(qct_ref.dtype)
    zt = _dot_nt(wt_ref[MOBA_W:2 * MOBA_W, :], h)
    vct = zt.astype(vct_ref.dtype)
    for r in range(nblk):
        qct_ref[0, r] = qct[:, r * MOBA_BLOCK:(r + 1) * MOBA_BLOCK]
        vct_ref[0, r] = vct[:, r * MOBA_BLOCK:(r + 1) * MOBA_BLOCK]


def _inproj_call(x2, mods_l, gain, wn, wt, ones_blk, gq_a, gk_a, gk_c, gq_c, b_sz, s_len):
    n, d = x2.shape
    tm = min(TM_IN, s_len)
    tpb = s_len // tm
    nblk = tm // MOBA_BLOCK
    nb = s_len // MOBA_BLOCK
    row = lambda i: (i, 0)
    const2 = lambda i: (0, 0)
    bf = MXU_DTYPE
    out_shape = (
        jax.ShapeDtypeStruct((n, SWA_Q), bf),
        jax.ShapeDtypeStruct((n, SWA_KV), bf),
        jax.ShapeDtypeStruct((n, SWA_KV), bf),
        jax.ShapeDtypeStruct((n, RNN_WIDTH), jnp.float32),
        jax.ShapeDtypeStruct((n, RNN_WIDTH), jnp.float32),
        jax.ShapeDtypeStruct((n, MOBA_W), bf),
        jax.ShapeDtypeStruct((n // tm, nblk, MOBA_W), jnp.float32),
        jax.ShapeDtypeStruct((n, 3 * D_MODEL), jnp.float32),
        jax.ShapeDtypeStruct((b_sz, nb, MOBA_W, MOBA_BLOCK), bf),
        jax.ShapeDtypeStruct((b_sz, nb, MOBA_W, MOBA_BLOCK), bf),
    )
    blk4 = pl.BlockSpec((1, nblk, MOBA_W, MOBA_BLOCK), lambda i: (i // tpb, i % tpb, 0, 0))
    out_specs = (
        pl.BlockSpec((tm, SWA_Q), row),
        pl.BlockSpec((tm, SWA_KV), row),
        pl.BlockSpec((tm, SWA_KV), row),
        pl.BlockSpec((tm, RNN_WIDTH), row),
        pl.BlockSpec((tm, RNN_WIDTH), row),
        pl.BlockSpec((tm, MOBA_W), row),
        pl.BlockSpec((1, nblk, MOBA_W), lambda i: (i, 0, 0)),
        pl.BlockSpec((tm, 3 * D_MODEL), row),
        blk4,
        blk4,
    )
    in_specs = [
        pl.BlockSpec((tm, d), row),
        pl.BlockSpec((1, 1, 6 * d), lambda i: (i // tpb, 0, 0)),
        pl.BlockSpec((1, d), const2),
        pl.BlockSpec(wn.shape, const2, pipeline_mode=pl.Buffered(1)),
        pl.BlockSpec(wt.shape, const2, pipeline_mode=pl.Buffered(1)),
        pl.BlockSpec(ones_blk.shape, const2),
        pl.BlockSpec(gq_a.shape, const2),
        pl.BlockSpec(gk_a.shape, const2),
        pl.BlockSpec(gk_c.shape, const2),
        pl.BlockSpec(gq_c.shape, const2),
    ]
    return pl.pallas_call(
        _inproj_kernel,
        grid=(n // tm,),
        in_specs=in_specs,
        out_specs=out_specs,
        out_shape=out_shape,
        compiler_params=_params(("arbitrary",)),
        name="inproj",
    )(x2, mods_l, gain, wn, wt, ones_blk, gq_a, gk_a, gk_c, gq_c)


def _swa_kernel(sink_ref, q_ref, kp_ref, ko_ref, vp_ref, vo_ref, bias_ref, o_ref):
    n = pl.program_id(1)
    L = SWA_BLOCK
    kcat = jnp.concatenate([kp_ref[0], ko_ref[0]], axis=0)
    vcat = jnp.concatenate([vp_ref[0], vo_ref[0]], axis=0)
    qi = lax.broadcasted_iota(jnp.int32, (L, 2 * L), 0)
    sj = lax.broadcasted_iota(jnp.int32, (L, 2 * L), 1)
    diff = qi + L - sj
    valid = (diff >= 0) & (diff < SWA_WINDOW) & ((n > 0) | (sj >= L))
    lane = lax.broadcasted_iota(jnp.int32, (L, LANES), 1)
    low = lane < HEAD_DIM
    zero = jnp.zeros((), q_ref.dtype)
    for c in range(SWA_Q // LANES):
        q2 = q_ref[0, :, c * LANES:(c + 1) * LANES]
        halves = []
        for g in range(SWA_KV_HEADS):
            head = g * SWA_GROUP + c
            qm = jnp.where(low if g == 0 else ~low, q2, zero)
            s = _dot_nt(qm, kcat) + bias_ref[head]
            s = jnp.where(valid, s, NEG_INF)
            sink = sink_ref[head]
            m = jnp.maximum(jnp.max(s, axis=-1, keepdims=True), sink)
            p = jnp.exp(s - m)
            denom = jnp.sum(p, axis=-1, keepdims=True) + jnp.exp(sink - m)
            p = (p / denom).astype(vcat.dtype)
            halves.append(_dot(p, vcat))
        o_ref[0, :, c * LANES:(c + 1) * LANES] = jnp.where(low, halves[0], halves[1]).astype(o_ref.dtype)


def _swa_call(qa, ka, va, sinks_l, bias_a, b_sz, s_len):
    L = SWA_BLOCK
    nblk = s_len // L
    q3 = qa.reshape(b_sz, s_len, SWA_Q)
    k3 = ka.reshape(b_sz, s_len, SWA_KV)
    v3 = va.reshape(b_sz, s_len, SWA_KV)
    prev = lambda b, n: (b, jnp.maximum(n - 1, 0), 0)
    own = lambda b, n: (b, n, 0)
    out = pl.pallas_call(
        _swa_kernel,
        grid=(b_sz, nblk),
        in_specs=[
            pl.BlockSpec(memory_space=pltpu.SMEM),
            pl.BlockSpec((1, L, SWA_Q), own),
            pl.BlockSpec((1, L, SWA_KV), prev),
            pl.BlockSpec((1, L, SWA_KV), own),
            pl.BlockSpec((1, L, SWA_KV), prev),
            pl.BlockSpec((1, L, SWA_KV), own),
            pl.BlockSpec((SWA_HEADS, L, 2 * L), lambda b, n: (0, 0, 0)),
        ],
        out_specs=pl.BlockSpec((1, L, SWA_Q), own),
        out_shape=jax.ShapeDtypeStruct((b_sz, s_len, SWA_Q), MXU_DTYPE),
        compiler_params=_params(("arbitrary", "arbitrary")),
        name="swa",
    )(sinks_l, q3, k3, k3, v3, v3, bias_a)
    return out.reshape(b_sz * s_len, SWA_Q)


def _gelu_tanh(x):
    return 0.5 * x * (1.0 + jnp.tanh(math.sqrt(2.0 / math.pi) * (x + 0.044715 * (x * x * x))))


def _rnn_kernel(xr_ref, yr_ref, cw_ref, cb_ref, wg_ref, bg_ref, lam_ref, o_ref, xbuf_ref, h_ref):
    t = pl.program_id(2)
    T = xr_ref.shape[1]
    C = xr_ref.shape[2]
    H = SUBLANES

    @pl.when(t == 0)
    def _():
        xbuf_ref[0:H, :] = jnp.zeros((H, C), jnp.float32)
        h_ref[...] = jnp.zeros(h_ref.shape, jnp.float32)

    x = xr_ref[0]
    xbuf_ref[H:H + T, :] = x
    xc = x * cw_ref[RNN_CONV - 1:RNN_CONV, :] + cb_ref[...]
    for j in range(1, RNN_CONV):
        xc = xc + xbuf_ref[H - j:H - j + T, :] * cw_ref[RNN_CONV - 1 - j:RNN_CONV - j, :]
    xbuf_ref[0:H, :] = x[T - H:T, :]

    gates = _dot(xc.astype(MXU_DTYPE), wg_ref[0]) + bg_ref[0]
    r = _sigmoid(gates[:, :C])
    i = _sigmoid(gates[:, C:])
    nlam = -lam_ref[...]
    softplus = jnp.maximum(nlam, 0.0) + jnp.log1p(jnp.exp(-jnp.abs(nlam)))
    log_a = (-RGLRU_C) * r * softplus
    a = jnp.exp(log_a)
    z2 = 2.0 * log_a
    a2 = a * a
    one = a2 == 1.0
    neg_expm1 = jnp.where(one, -z2, (1.0 - a2) * z2 / jnp.where(one, 1.0, jnp.log(a2)))
    mult = jnp.sqrt(neg_expm1)
    row = lax.broadcasted_iota(jnp.int32, (T, C), 0)
    mult = jnp.where((row == 0) & (t == 0), 1.0, mult)
    u = mult * (i * xc)

    step = 1
    while step < T:
        keep = row >= step
        a_prev = jnp.where(keep, pltpu.roll(a, step, 0), 1.0)
        u_prev = jnp.where(keep, pltpu.roll(u, step, 0), 0.0)
        u = a * u_prev + u
        a = a * a_prev
        step *= 2
    hs = u + a * h_ref[0:1, :]
    h_ref[0:1, :] = hs[T - 1:T, :]
    o_ref[0] = (hs * _gelu_tanh(yr_ref[0])).astype(o_ref.dtype)


def _rnn_call(xr, yr, conv_w, conv_b, wg, bg, lam, b_sz, s_len):
    T = min(T_RNN, s_len)
    C = C_RNN
    nc = RNN_WIDTH // C
    x3 = xr.reshape(b_sz, s_len, RNN_WIDTH)
    y3 = yr.reshape(b_sz, s_len, RNN_WIDTH)
    tile = lambda b, c, t: (b, t, c)
    chan = lambda b, c, t: (0, c)
    out = pl.pallas_call(
        _rnn_kernel,
        grid=(b_sz, nc, s_len // T),
        in_specs=[
            pl.BlockSpec((1, T, C), tile),
            pl.BlockSpec((1, T, C), tile),
            pl.BlockSpec((RNN_CONV, C), chan),
            pl.BlockSpec((1, C), chan),
            pl.BlockSpec((1, C, 2 * C), lambda b, c, t: (c, 0, 0)),
            pl.BlockSpec((1, 1, 2 * C), lambda b, c, t: (c, 0, 0)),
            pl.BlockSpec((1, C), chan),
        ],
        out_specs=pl.BlockSpec((1, T, C), tile),
        out_shape=jax.ShapeDtypeStruct((b_sz, s_len, RNN_WIDTH), MXU_DTYPE),
        scratch_shapes=[
            pltpu.VMEM((T + SUBLANES, C), jnp.float32),
            pltpu.VMEM((SUBLANES, C), jnp.float32),
        ],
        compiler_params=_params(("arbitrary", "arbitrary", "arbitrary")),
        name="rglru",
    )(x3, y3, conv_w, conv_b, wg, bg, lam)
    return out.reshape(b_sz * s_len, RNN_WIDTH)


def _moba_kernel(far_ref, qt_ref, k_ref, vt_ref, kmean_ref, bias_ref, o_ref, sel_ref, m_ref, l_ref, acc_ref):
    p_idx = pl.program_id(1)
    n = pl.program_id(2)
    MB = MOBA_BLOCK
    nb = kmean_ref.shape[1]
    qt = qt_ref[0, 0]
    rows = lax.broadcasted_iota(jnp.int32, (LANES, MB), 0)
    zero = jnp.zeros((), qt.dtype)
    qts = [jnp.where(rows < HEAD_DIM, qt, zero), jnp.where(rows >= HEAD_DIM, qt, zero)]

    kmean = kmean_ref[0].astype(MXU_DTYPE)
    blk = lax.broadcasted_iota(jnp.int32, (nb, MB), 0).astype(jnp.float32)
    past = blk < n.astype(jnp.float32)
    for e in range(2):
        gate = _dot(kmean, qts[e])
        gate = jnp.where(past, gate, NEG_INF)
        sel = jnp.zeros((nb, MB), jnp.float32)
        for _ in range(MOBA_TOPK):
            mx = jnp.max(gate, axis=0, keepdims=True)
            first = jnp.min(jnp.where(gate == mx, blk, float(nb)), axis=0, keepdims=True)
            hit = blk == first
            sel = jnp.where(hit, 1.0, sel)
            gate = jnp.where(hit, TAKEN, gate)
        sel_ref[e] = jnp.where(past, sel, 0.0)

    kj = lax.broadcasted_iota(jnp.int32, (MB, MB), 0)
    qi = lax.broadcasted_iota(jnp.int32, (MB, MB), 1)
    causal = kj <= qi
    k_own = k_ref[0, pl.ds(pl.multiple_of(n * MB, MB), MB), :]
    vt_own = vt_ref[0, n]
    for e in range(2):
        s = _dot(k_own, qts[e]) + bias_ref[e, 0]
        s = jnp.where(causal, s, NEG_INF)
        mx = jnp.max(s, axis=0, keepdims=True)
        p = jnp.exp(s - mx)
        m_ref[e] = mx
        l_ref[e] = jnp.sum(p, axis=0, keepdims=True)
        acc_ref[e] = _dot(vt_own, p.astype(vt_own.dtype))

    def block_step(m, bias_of):
        k_m = k_ref[0, pl.ds(pl.multiple_of(m * MB, MB), MB), :]
        vt_m = vt_ref[0, m]
        for e in range(2):
            s = _dot(k_m, qts[e]) + bias_of(e, m)
            chosen = sel_ref[e, pl.ds(m, 1), :] > 0.0
            s = jnp.where(chosen, s, NEG_INF)
            m_old = m_ref[e]
            m_new = jnp.maximum(m_old, jnp.max(s, axis=0, keepdims=True))
            alpha = jnp.exp(m_old - m_new)
            p = jnp.exp(s - m_new)
            m_ref[e] = m_new
            l_ref[e] = alpha * l_ref[e] + jnp.sum(p, axis=0, keepdims=True)
            acc_ref[e] = alpha * acc_ref[e] + _dot(vt_m, p.astype(vt_m.dtype))

    n_far = jnp.maximum(n - (MOBA_NEAR - 1), 0)

    def far_body(m, carry):
        block_step(m, lambda e, m_: far_ref[2 * p_idx + e])
        return carry

    lax.fori_loop(0, n_far, far_body, 0)

    def near_body(m, carry):
        block_step(m, lambda e, m_: bias_ref[e, n - m_])
        return carry

    lax.fori_loop(n_far, n, near_body, 0)

    top = acc_ref[0, 0:HEAD_DIM, :] / l_ref[0]
    bot = acc_ref[1, HEAD_DIM:LANES, :] / l_ref[1]
    o_ref[0] = jnp.concatenate([top, bot], axis=0).T.astype(o_ref.dtype)


def _moba_call(qct, kc, vct, kmean, bias_c, far_c, b_sz, s_len):
    MB = MOBA_BLOCK
    nb = s_len // MB
    k3 = kc.reshape(b_sz, s_len, MOBA_W)
    km3 = kmean.reshape(b_sz, nb, MOBA_W)
    npair = MOBA_W // LANES
    out = pl.pallas_call(
        _moba_kernel,
        grid=(b_sz, npair, nb),
        in_specs=[
            pl.BlockSpec(memory_space=pltpu.SMEM),
            pl.BlockSpec((1, 1, LANES, MB), lambda b, p, n: (b, n, p, 0)),
            pl.BlockSpec((1, s_len, LANES), lambda b, p, n: (b, 0, p)),
            pl.BlockSpec((1, nb, LANES, MB), lambda b, p, n: (b, 0, p, 0)),
            pl.BlockSpec((1, nb, LANES), lambda b, p, n: (b, 0, p)),
            pl.BlockSpec((2, MOBA_NEAR, MB, MB), lambda b, p, n: (p, 0, 0, 0)),
        ],
        out_specs=pl.BlockSpec((1, MB, LANES), lambda b, p, n: (b, n, p)),
        out_shape=jax.ShapeDtypeStruct((b_sz, s_len, MOBA_W), MXU_DTYPE),
        scratch_shapes=[
            pltpu.VMEM((2, nb, MB), jnp.float32),
            pltpu.VMEM((2, 1, MB), jnp.float32),
            pltpu.VMEM((2, 1, MB), jnp.float32),
            pltpu.VMEM((2, LANES, MB), jnp.float32),
        ],
        compiler_params=_params(("arbitrary", "arbitrary", "arbitrary")),
        name="moba",
    )(far_c, qct, k3, vct, km3, bias_c)
    return out.reshape(b_sz * s_len, MOBA_W)


def _merge_kernel(x_ref, oa_ref, ob_ref, oc_ref, g_ref, mod_ref, wa_ref, wb_ref, wc_ref, wo_ref, o_ref):
    d = D_MODEL
    merged = _sigmoid(g_ref[:, 0:d]) * _dot(oa_ref[...], wa_ref[...])
    merged = merged + _sigmoid(g_ref[:, d:2 * d]) * _dot(ob_ref[...], wb_ref[...])
    merged = merged + _sigmoid(g_ref[:, 2 * d:3 * d]) * _dot(oc_ref[...], wc_ref[...])
    y = _dot(merged.astype(MXU_DTYPE), wo_ref[...])
    gate_m = mod_ref[0, :, 2 * d:3 * d]
    o_ref[...] = x_ref[...] + gate_m * y


def _merge_call(x2, o_a, o_b, o_c, g_logits, mods_l, wa, wb, wc, wo, s_len):
    n, d = x2.shape
    tm = min(TM_MERGE, s_len)
    tpb = s_len // tm
    row = lambda i: (i, 0)
    const2 = lambda i: (0, 0)
    return pl.pallas_call(
        _merge_kernel,
        grid=(n // tm,),
        in_specs=[
            pl.BlockSpec((tm, d), row),
            pl.BlockSpec((tm, SWA_Q), row),
            pl.BlockSpec((tm, RNN_WIDTH), row),
            pl.BlockSpec((tm, MOBA_W), row),
            pl.BlockSpec((tm, 3 * d), row),
            pl.BlockSpec((1, 1, 6 * d), lambda i: (i // tpb, 0, 0)),
            pl.BlockSpec(wa.shape, const2),
            pl.BlockSpec(wb.shape, const2),
            pl.BlockSpec(wc.shape, const2),
            pl.BlockSpec(wo.shape, const2),
        ],
        out_specs=pl.BlockSpec((tm, d), row),
        out_shape=jax.ShapeDtypeStruct((n, d), jnp.float32),
        compiler_params=_params(("arbitrary",)),
        name="merge",
    )(x2, o_a, o_b, o_c, g_logits, mods_l, wa, wb, wc, wo)


def _ffn_kernel(x_ref, mod_ref, gain_ref, wug_ref, wuv_ref, cwg_ref, cwv_ref, cbg_ref, cbv_ref, wd_ref,
                o_ref, h_ref, acc_ref, ubuf_ref, halo_ref, *, tiles_per_seq):
    i = pl.program_id(0)
    j = pl.program_id(1)
    nj = pl.num_programs(1)
    d = D_MODEL
    tm = x_ref.shape[0]
    H = SUBLANES

    @pl.when(j == 0)
    def _():
        x = x_ref[...]
        shift = mod_ref[0, :, 3 * d:4 * d]
        scale = mod_ref[0, :, 4 * d:5 * d]
        ms = jnp.mean(x * x, axis=-1, keepdims=True)
        y = x * lax.rsqrt(ms + RMS_EPS) * gain_ref[...]
        h_ref[...] = (y * (1.0 + scale) + shift).astype(h_ref.dtype)
        acc_ref[...] = jnp.zeros(acc_ref.shape, jnp.float32)

    @pl.when((i == 0) & (j == 0))
    def _():
        halo_ref[...] = jnp.zeros(halo_ref.shape, jnp.float32)

    seq_start = (i % tiles_per_seq) == 0
    h = h_ref[...]

    def conv(w_ref, cw_ref, cb_ref, slot):
        u = _dot(h, w_ref[...])
        prev = halo_ref[slot, j]
        ubuf_ref[slot, 0:H, :] = jnp.where(seq_start, 0.0, prev)
        ubuf_ref[slot, H:H + tm, :] = u
        halo_ref[slot, j] = u[tm - H:tm, :]
        out = u * cw_ref[FFN_CONV - 1:FFN_CONV, :] + cb_ref[...]
        for k in range(1, FFN_CONV):
            out = out + ubuf_ref[slot, H - k:H - k + tm, :] * cw_ref[FFN_CONV - 1 - k:FFN_CONV - k, :]
        return out

    ug = conv(wug_ref, cwg_ref, cbg_ref, 0)
    uv = conv(wuv_ref, cwv_ref, cbv_ref, 1)
    act = (ug * _sigmoid(ug) * uv).astype(MXU_DTYPE)
    acc_ref[...] += _dot(act, wd_ref[...])

    @pl.when(j == nj - 1)
    def _():
        gate_f = mod_ref[0, :, 5 * d:6 * d]
        o_ref[...] = x_ref[...] + gate_f * acc_ref[...]


def _ffn_call(x2, mods_l, gain, w_up, conv_w, conv_b, w_down, s_len):
    n, d = x2.shape
    tm = min(TM_FFN, s_len)
    tn = TN_FFN
    nj = D_FF // tn
    tpb = s_len // tm
    row = lambda i, j: (i, 0)
    kern = functools.partial(_ffn_kernel, tiles_per_seq=tpb)
    return pl.pallas_call(
        kern,
        grid=(n // tm, nj),
        in_specs=[
            pl.BlockSpec((tm, d), row),
            pl.BlockSpec((1, 1, 6 * d), lambda i, j: (i // tpb, 0, 0)),
            pl.BlockSpec((1, d), lambda i, j: (0, 0)),
            pl.BlockSpec((d, tn), lambda i, j: (0, j)),
            pl.BlockSpec((d, tn), lambda i, j: (0, nj + j)),
            pl.BlockSpec((FFN_CONV, tn), lambda i, j: (0, j)),
            pl.BlockSpec((FFN_CONV, tn), lambda i, j: (0, nj + j)),
            pl.BlockSpec((1, tn), lambda i, j: (0, j)),
            pl.BlockSpec((1, tn), lambda i, j: (0, nj + j)),
            pl.BlockSpec((tn, d), lambda i, j: (j, 0)),
        ],
        out_specs=pl.BlockSpec((tm, d), row),
        out_shape=jax.ShapeDtypeStruct((n, d), jnp.float32),
        scratch_shapes=[
            pltpu.VMEM((tm, d), MXU_DTYPE),
            pltpu.VMEM((tm, d), jnp.float32),
            pltpu.VMEM((2, tm + SUBLANES, tn), jnp.float32),
            pltpu.VMEM((2, nj, SUBLANES, tn), jnp.float32),
        ],
        compiler_params=_params(("arbitrary", "arbitrary")),
        name="ffn",
    )(x2, mods_l, gain, w_up, w_up, conv_w, conv_w, conv_b, conv_b, w_down)


def _swa_head_order():
    order = []
    for c in range(SWA_Q // LANES):
        for g in range(SWA_KV_HEADS):
            order.append(g * SWA_GROUP + c)
    return order


def _bias_tables(rel_bias):
    bias_a = rel_bias[:, :SWA_HEADS]
    bias_c = rel_bias[:, SWA_HEADS:]
    L = SWA_BLOCK
    qi = np.arange(L)[:, None]
    sj = np.arange(2 * L)[None, :]
    bucket_a = _t5_bucket_np(qi + L - sj)
    tab_a = jnp.transpose(bias_a[bucket_a], (2, 0, 1))
    MB = MOBA_BLOCK
    kj = np.arange(MB)[:, None]
    qq = np.arange(MB)[None, :]
    deltas = np.arange(MOBA_NEAR)[:, None, None]
    bucket_c = _t5_bucket_np(deltas * MB + qq[None] - kj[None])
    tab_c = jnp.transpose(bias_c[bucket_c], (3, 0, 1, 2))
    far_c = bias_c[NUM_BUCKETS - 1, :]
    return tab_a.astype(jnp.float32), tab_c.astype(jnp.float32), far_c.astype(jnp.float32)


def kernel(x, c, w_mod, b_mod, norm_mix, norm_ffn, w_in, qnorm_a, knorm_a, sinks, rnn_conv_w, rnn_conv_b,
           rnn_gate_a_w, rnn_gate_a_b, rnn_gate_x_w, rnn_gate_x_b, rnn_lambda, qnorm_c, knorm_c, rel_bias,
           w_branch, w_out, w_up, ffn_conv_w, ffn_conv_b, w_down):
    b_sz, s_len, d = x.shape
    depth = w_in.shape[0]
    assert d == D_MODEL and s_len % MOBA_BLOCK == 0 and b_sz <= SUBLANES
    n = b_sz * s_len
    bf = MXU_DTYPE
    f32 = jnp.float32

    c_pad = jnp.zeros((SUBLANES, d), f32).at[:b_sz].set(c)
    mods = _mod_call(c_pad, w_mod, b_mod)[:, :b_sz]
    mods = mods.reshape(depth, b_sz, 1, 6 * d)

    offs = np.cumsum([0, SWA_Q, SWA_KV, SWA_KV, RNN_WIDTH, RNN_WIDTH, MOBA_W, MOBA_W, MOBA_W, 3 * d])
    o_qa, o_ka, o_va, o_xr, o_yr, o_qc, o_kc, o_vc, o_g, o_end = [int(v) for v in offs]
    head_order = _swa_head_order()
    qa_cols = np.concatenate([np.arange(h * HEAD_DIM, (h + 1) * HEAD_DIM) for h in head_order])
    cols_n = np.concatenate([o_qa + qa_cols, np.arange(o_ka, o_qc), np.arange(o_kc, o_vc), np.arange(o_g, o_end)])
    cols_t = np.concatenate([np.arange(o_qc, o_kc), np.arange(o_vc, o_g)])
    wn_all = w_in[:, :, cols_n].astype(bf)
    wt_all = jnp.swapaxes(w_in[:, :, cols_t], 1, 2).astype(bf)
    lane_head = np.arange(LANES) // HEAD_DIM
    ones_blk = jnp.asarray(lane_head[:, None] == lane_head[None, :], bf)

    wa_all = w_branch[:, :SWA_Q][:, qa_cols].astype(bf)
    wb_all = w_branch[:, SWA_Q:SWA_Q + RNN_WIDTH].astype(bf)
    wc_all = w_branch[:, SWA_Q + RNN_WIDTH:].astype(bf)
    wo_all = w_out.astype(bf)
    wup_all = w_up.astype(bf)
    wdn_all = w_down.astype(bf)

    per = C_RNN // RNN_BLOCK_WIDTH
    nct = RNN_WIDTH // C_RNN
    eye = jnp.eye(per, dtype=f32)

    def blockdiag(w):
        w5 = w.reshape(depth, nct, per, RNN_BLOCK_WIDTH, RNN_BLOCK_WIDTH)
        full = jnp.einsum('lcpij,pq->lcpiqj', w5, eye)
        return full.reshape(depth, nct, C_RNN, C_RNN)

    wg_all = jnp.concatenate([blockdiag(rnn_gate_a_w), blockdiag(rnn_gate_x_w)], axis=-1).astype(bf)
    bg_all = jnp.concatenate([rnn_gate_a_b.reshape(depth, nct, 1, C_RNN),
                              rnn_gate_x_b.reshape(depth, nct, 1, C_RNN)], axis=-1)

    tab_a, tab_c, far_c = _bias_tables(rel_bias)

    x2 = x.reshape(n, d)
    for l in range(depth):
        gq_a = jnp.tile(qnorm_a[l], SWA_Q // HEAD_DIM)[None, :]
        gk_a = jnp.tile(knorm_a[l], SWA_KV // HEAD_DIM)[None, :]
        gk_c = jnp.tile(knorm_c[l], MOBA_W // HEAD_DIM)[None, :]
        gq_c = jnp.tile(qnorm_c[l], MOBA_W // HEAD_DIM)[:, None]
        qa, ka, va, xr, yr, kc, kmean, g_logits, qct, vct = _inproj_call(
            x2, mods[l], norm_mix[l][None, :], wn_all[l], wt_all[l], ones_blk, gq_a, gk_a, gk_c, gq_c,
            b_sz, s_len)
        o_a = _swa_call(qa, ka, va, sinks[l], tab_a, b_sz, s_len)
        o_b = _rnn_call(xr, yr, rnn_conv_w[l], rnn_conv_b[l][None, :], wg_all[l], bg_all[l],
                        rnn_lambda[l][None, :], b_sz, s_len)
        o_c = _moba_call(qct, kc, vct, kmean, tab_c, far_c, b_sz, s_len)
        x2 = _merge_call(x2, o_a, o_b, o_c, g_logits, mods[l], wa_all[l], wb_all[l], wc_all[l], wo_all[l],
                         s_len)
        x2 = _ffn_call(x2, mods[l], norm_ffn[l][None, :], wup_all[l], ffn_conv_w[l], ffn_conv_b[l][None, :],
                       wdn_all[l], s_len)
    return x2.reshape(b_sz, s_len, d)
```

```python
import functools
import math

import numpy as np
import jax
import jax.numpy as jnp
from jax import lax
from jax.experimental import pallas as pl
from jax.experimental.pallas import tpu as pltpu

D_MODEL = 1024
DEPTH = 4
HEAD_DIM = 64
SWA_HEADS = 8
SWA_KV_HEADS = 2
SWA_WINDOW = 128
SWA_BLOCK = 128
RNN_WIDTH = D_MODEL
RNN_BLOCKS = 16
RNN_BLOCK_WIDTH = RNN_WIDTH // RNN_BLOCKS
RNN_CONV = 4
RGLRU_C = 8.0
MOBA_HEADS = 8
MOBA_BLOCK = 256
MOBA_TOPK = 3
NUM_BUCKETS = 32
MAX_DISTANCE = 2048
D_FF = 2816
FFN_CONV = 3
RMS_EPS = 1e-6
NEG_INF = -1e30
TAKEN = -3e38
SWA_Q = SWA_HEADS * HEAD_DIM
SWA_KV = SWA_KV_HEADS * HEAD_DIM
MOBA_W = MOBA_HEADS * HEAD_DIM
SWA_GROUP = SWA_HEADS // SWA_KV_HEADS

LANES = 128
SUBLANES = 8
VMEM_LIMIT_BYTES = 56 * 1024 * 1024

MXU_DTYPE = jnp.bfloat16
SM_SCALE = HEAD_DIM ** -0.5

TM_IN = 512
TM_MERGE = 512
TM_FFN = 512
FFN_CHUNK = 256
T_RNN = 256
C_RNN = 256


def _params(semantics):
    return pltpu.CompilerParams(dimension_semantics=semantics, vmem_limit_bytes=VMEM_LIMIT_BYTES)


def _dot(a, b):
    return jnp.dot(a, b, preferred_element_type=jnp.float32)


def _dot_nt(a, b):
    return lax.dot_general(a, b, (((1,), (1,)), ((), ())), preferred_element_type=jnp.float32)


def _sigmoid(x):
    return 1.0 / (1.0 + jnp.exp(-x))


def _t5_bucket_np(dist):
    dist = np.maximum(dist, 0)
    max_exact = NUM_BUCKETS // 2
    ratio = np.maximum(dist, 1).astype(np.float32) / np.float32(max_exact)
    log_ratio = np.log(ratio).astype(np.float32) / np.float32(math.log(MAX_DISTANCE / max_exact))
    large = max_exact + (log_ratio * np.float32(NUM_BUCKETS - max_exact)).astype(np.int32)
    large = np.minimum(large, NUM_BUCKETS - 1)
    return np.where(dist < max_exact, dist, large).astype(np.int32)


def _moba_near_blocks():
    delta = 0
    while True:
        lo = max(delta * MOBA_BLOCK - (MOBA_BLOCK - 1), 0)
        if int(_t5_bucket_np(np.array([lo]))[0]) == NUM_BUCKETS - 1:
            return delta
        delta += 1


MOBA_NEAR = _moba_near_blocks()
MOBA_SUM_ROWS = 2 * SUBLANES


def _mod_kernel(c_ref, w_ref, b_ref, o_ref):
    c = c_ref[...]
    c_act = (c * _sigmoid(c)).astype(MXU_DTYPE)
    o_ref[0] = _dot(c_act, w_ref[0].astype(MXU_DTYPE)) + b_ref[0]


def _mod_call(c_pad, w_mod, b_mod):
    depth, d, d6 = w_mod.shape
    tn = 1536
    return pl.pallas_call(
        _mod_kernel,
        grid=(depth, d6 // tn),
        in_specs=[
            pl.BlockSpec((SUBLANES, d), lambda l, j: (0, 0)),
            pl.BlockSpec((1, d, tn), lambda l, j: (l, 0, j)),
            pl.BlockSpec((1, 1, tn), lambda l, j: (l, 0, j)),
        ],
        out_specs=pl.BlockSpec((1, SUBLANES, tn), lambda l, j: (l, 0, j)),
        out_shape=jax.ShapeDtypeStruct((depth, SUBLANES, d6), jnp.float32),
        compiler_params=_params(("arbitrary", "arbitrary")),
        name="adaln_mod",
    )(c_pad, w_mod, b_mod.reshape(depth, 1, d6))


_C_QA = 0
_C_KVA = _C_QA + SWA_Q
_C_XR = _C_KVA + 2 * SWA_KV
_C_YR = _C_XR + RNN_WIDTH
_C_KC = _C_YR + RNN_WIDTH
_C_G = _C_KC + MOBA_W
_C_END = _C_G + 3 * D_MODEL


def _head_norm_rows(z, ones_blk, gain):
    zz = z * z
    hi = zz.astype(jnp.bfloat16)
    lo = (zz - hi.astype(jnp.float32)).astype(jnp.bfloat16)
    outs = []
    for c in range(z.shape[1] // LANES):
        sl = slice(c * LANES, (c + 1) * LANES)
        ss = _dot(hi[:, sl], ones_blk) + _dot(lo[:, sl], ones_blk)
        outs.append(z[:, sl] * lax.rsqrt(ss * (1.0 / HEAD_DIM) + RMS_EPS))
    y = outs[0] if len(outs) == 1 else jnp.concatenate(outs, axis=1)
    return y * gain


def _head_norm_cols(zt, gain_col):
    w, m = zt.shape
    z3 = zt.reshape(w // HEAD_DIM, HEAD_DIM, m)
    ss = jnp.sum(z3 * z3, axis=1, keepdims=True)
    y = z3 * lax.rsqrt(ss * (1.0 / HEAD_DIM) + RMS_EPS)
    return y.reshape(w, m) * gain_col


def _inproj_kernel(x_ref, mod_ref, gain_ref, wn_ref, wt_ref, ones_ref, gq_a_ref, gk_a_ref, gk_c_ref,
                   gq_c_ref, qa_ref, ka_ref, va_ref, xr_ref, yr_ref, kc_ref, kmean_ref, g_ref,
                   qct_ref, vct_ref):
    d = D_MODEL
    x = x_ref[...]
    shift = mod_ref[0, :, 0:d]
    scale = mod_ref[0, :, d:2 * d]
    ms = jnp.mean(x * x, axis=-1, keepdims=True)
    y = x * lax.rsqrt(ms + RMS_EPS) * gain_ref[...]
    h = (y * (1.0 + scale) + shift).astype(MXU_DTYPE)
    ones_blk = ones_ref[...]

    z = _dot(h, wn_ref[:, _C_QA:_C_KVA])
    qa_ref[...] = (_head_norm_rows(z, ones_blk, gq_a_ref[...]) * SM_SCALE).astype(qa_ref.dtype)

    z = _dot(h, wn_ref[:, _C_KVA:_C_XR])
    ka_ref[...] = _head_norm_rows(z[:, :SWA_KV], ones_blk, gk_a_ref[...]).astype(ka_ref.dtype)
    va_ref[...] = z[:, SWA_KV:].astype(va_ref.dtype)

    xr_ref[...] = _dot(h, wn_ref[:, _C_XR:_C_YR])
    yr_ref[...] = _dot(h, wn_ref[:, _C_YR:_C_KC])

    z = _dot(h, wn_ref[:, _C_KC:_C_G])
    kc = _head_norm_rows(z, ones_blk, gk_c_ref[...])
    kc_ref[...] = kc.astype(kc_ref.dtype)
    tm = x.shape[0]
    nblk = tm // MOBA_BLOCK
    kmean_ref[0] = jnp.mean(kc.reshape(nblk, MOBA_BLOCK, MOBA_W), axis=1)

    for c in range(3 * D_MODEL // 512):
        g_ref[:, c * 512:(c + 1) * 512] = _dot(h, wn_ref[:, _C_G + c * 512:_C_G + (c + 1) * 512])

    zt = _dot_nt(wt_ref[0:MOBA_W, :], h)
    qct = (_head_norm_cols(zt, gq_c_ref[...]) * SM_SCALE).astype(qct_ref.dtype)
    zt = _dot_nt(wt_ref[MOBA_W:2 * MOBA_W, :], h)
    vct = zt.astype(vct_ref.dtype)
    for r in range(nblk):
        qct_ref[0, r] = qct[:, r * MOBA_BLOCK:(r + 1) * MOBA_BLOCK]
        vct_ref[0, r] = vct[:, r * MOBA_BLOCK:(r + 1) * MOBA_BLOCK]


def _inproj_call(x2, mods_l, gain, wn, wt, ones_blk, gq_a, gk_a, gk_c, gq_c, b_sz, s_len):
    n, d = x2.shape
    tm = min(TM_IN, s_len)
    tpb = s_len // tm
    nblk = tm // MOBA_BLOCK
    nb = s_len // MOBA_BLOCK
    row = lambda i: (i, 0)
    const2 = lambda i: (0, 0)
    bf = MXU_DTYPE
    out_shape = (
        jax.ShapeDtypeStruct((n, SWA_Q), bf),
        jax.ShapeDtypeStruct((n, SWA_KV), bf),
        jax.ShapeDtypeStruct((n, SWA_KV), bf),
        jax.ShapeDtypeStruct((n, RNN_WIDTH), jnp.float32),
        jax.ShapeDtypeStruct((n, RNN_WIDTH), jnp.float32),
        jax.ShapeDtypeStruct((n, MOBA_W), bf),
        jax.ShapeDtypeStruct((n // tm, nblk, MOBA_W), jnp.float32),
        jax.ShapeDtypeStruct((n, 3 * D_MODEL), jnp.float32),
        jax.ShapeDtypeStruct((b_sz, nb, MOBA_W, MOBA_BLOCK), bf),
        jax.ShapeDtypeStruct((b_sz, nb, MOBA_W, MOBA_BLOCK), bf),
    )
    blk4 = pl.BlockSpec((1, nblk, MOBA_W, MOBA_BLOCK), lambda i: (i // tpb, i % tpb, 0, 0))
    out_specs = (
        pl.BlockSpec((tm, SWA_Q), row),
        pl.BlockSpec((tm, SWA_KV), row),
        pl.BlockSpec((tm, SWA_KV), row),
        pl.BlockSpec((tm, RNN_WIDTH), row),
        pl.BlockSpec((tm, RNN_WIDTH), row),
        pl.BlockSpec((tm, MOBA_W), row),
        pl.BlockSpec((1, nblk, MOBA_W), lambda i: (i, 0, 0)),
        pl.BlockSpec((tm, 3 * D_MODEL), row),
        blk4,
        blk4,
    )
    in_specs = [
        pl.BlockSpec((tm, d), row),
        pl.BlockSpec((1, 1, 6 * d), lambda i: (i // tpb, 0, 0)),
        pl.BlockSpec((1, d), const2),
        pl.BlockSpec(wn.shape, const2, pipeline_mode=pl.Buffered(1)),
        pl.BlockSpec(wt.shape, const2, pipeline_mode=pl.Buffered(1)),
        pl.BlockSpec(ones_blk.shape, const2),
        pl.BlockSpec(gq_a.shape, const2),
        pl.BlockSpec(gk_a.shape, const2),
        pl.BlockSpec(gk_c.shape, const2),
        pl.BlockSpec(gq_c.shape, const2),
    ]
    return pl.pallas_call(
        _inproj_kernel,
        grid=(n // tm,),
        in_specs=in_specs,
        out_specs=out_specs,
        out_shape=out_shape,
        compiler_params=_params(("arbitrary",)),
        name="inproj",
    )(x2, mods_l, gain, wn, wt, ones_blk, gq_a, gk_a, gk_c, gq_c)


def _swa_kernel(sink_ref, q_ref, kp_ref, ko_ref, vp_ref, vo_ref, bias_ref, o_ref):
    n = pl.program_id(1)
    L = SWA_BLOCK
    kcat = jnp.concatenate([kp_ref[0], ko_ref[0]], axis=0)
    vcat = jnp.concatenate([vp_ref[0], vo_ref[0]], axis=0)
    qi = lax.broadcasted_iota(jnp.int32, (L, 2 * L), 0)
    sj = lax.broadcasted_iota(jnp.int32, (L, 2 * L), 1)
    diff = qi + L - sj
    valid = (diff >= 0) & (diff < SWA_WINDOW) & ((n > 0) | (sj >= L))
    lane = lax.broadcasted_iota(jnp.int32, (L, LANES), 1)
    low = lane < HEAD_DIM
    zero = jnp.zeros((), q_ref.dtype)
    for c in range(SWA_Q // LANES):
        q2 = q_ref[0, :, c * LANES:(c + 1) * LANES]
        halves = []
        for g in range(SWA_KV_HEADS):
            head = g * SWA_GROUP + c
            qm = jnp.where(low if g == 0 else ~low, q2, zero)
            s = _dot_nt(qm, kcat) + bias_ref[head]
            s = jnp.where(valid, s, NEG_INF)
            sink = sink_ref[head]
            m = jnp.maximum(jnp.max(s, axis=-1, keepdims=True), sink)
            p = jnp.exp(s - m)
            denom = jnp.sum(p, axis=-1, keepdims=True) + jnp.exp(sink - m)
            p = (p / denom).astype(vcat.dtype)
            halves.append(_dot(p, vcat))
        o_ref[0, :, c * LANES:(c + 1) * LANES] = jnp.where(low, halves[0], halves[1]).astype(o_ref.dtype)


def _swa_call(qa, ka, va, sinks_l, bias_a, b_sz, s_len):
    L = SWA_BLOCK
    nblk = s_len // L
    q3 = qa.reshape(b_sz, s_len, SWA_Q)
    k3 = ka.reshape(b_sz, s_len, SWA_KV)
    v3 = va.reshape(b_sz, s_len, SWA_KV)
    prev = lambda b, n: (b, jnp.maximum(n - 1, 0), 0)
    own = lambda b, n: (b, n, 0)
    out = pl.pallas_call(
        _swa_kernel,
        grid=(b_sz, nblk),
        in_specs=[
            pl.BlockSpec(memory_space=pltpu.SMEM),
            pl.BlockSpec((1, L, SWA_Q), own),
            pl.BlockSpec((1, L, SWA_KV), prev),
            pl.BlockSpec((1, L, SWA_KV), own),
            pl.BlockSpec((1, L, SWA_KV), prev),
            pl.BlockSpec((1, L, SWA_KV), own),
            pl.BlockSpec((SWA_HEADS, L, 2 * L), lambda b, n: (0, 0, 0)),
        ],
        out_specs=pl.BlockSpec((1, L, SWA_Q), own),
        out_shape=jax.ShapeDtypeStruct((b_sz, s_len, SWA_Q), MXU_DTYPE),
        compiler_params=_params(("arbitrary", "arbitrary")),
        name="swa",
    )(sinks_l, q3, k3, k3, v3, v3, bias_a)
    return out.reshape(b_sz * s_len, SWA_Q)


def _gelu_tanh(x):
    return 0.5 * x * (1.0 + jnp.tanh(math.sqrt(2.0 / math.pi) * (x + 0.044715 * (x * x * x))))


def _rnn_kernel(xr_ref, yr_ref, cw_ref, cb_ref, wg_ref, bg_ref, lam_ref, o_ref, xbuf_ref, h_ref):
    t = pl.program_id(2)
    T = xr_ref.shape[1]
    C = xr_ref.shape[2]
    H = SUBLANES

    @pl.when(t == 0)
    def _():
        xbuf_ref[0:H, :] = jnp.zeros((H, C), jnp.float32)
        h_ref[...] = jnp.zeros(h_ref.shape, jnp.float32)

    x = xr_ref[0]
    xbuf_ref[H:H + T, :] = x
    xc = x * cw_ref[RNN_CONV - 1:RNN_CONV, :] + cb_ref[...]
    for j in range(1, RNN_CONV):
        xc = xc + xbuf_ref[H - j:H - j + T, :] * cw_ref[RNN_CONV - 1 - j:RNN_CONV - j, :]
    xbuf_ref[0:H, :] = x[T - H:T, :]

    gates = _dot(xc.astype(MXU_DTYPE), wg_ref[0]) + bg_ref[0]
    r = _sigmoid(gates[:, :C])
    i = _sigmoid(gates[:, C:])
    nlam = -lam_ref[...]
    softplus = jnp.maximum(nlam, 0.0) + jnp.log1p(jnp.exp(-jnp.abs(nlam)))
    log_a = (-RGLRU_C) * r * softplus
    a = jnp.exp(log_a)
    z2 = 2.0 * log_a
    a2 = a * a
    one = a2 == 1.0
    neg_expm1 = jnp.where(one, -z2, (1.0 - a2) * z2 / jnp.where(one, 1.0, jnp.log(a2)))
    mult = jnp.sqrt(neg_expm1)
    row = lax.broadcasted_iota(jnp.int32, (T, C), 0)
    mult = jnp.where((row == 0) & (t == 0), 1.0, mult)
    u = mult * (i * xc)

    step = 1
    while step < T:
        keep = row >= step
        a_prev = jnp.where(keep, pltpu.roll(a, step, 0), 1.0)
        u_prev = jnp.where(keep, pltpu.roll(u, step, 0), 0.0)
        u = a * u_prev + u
        a = a * a_prev
        step *= 2
    hs = u + a * h_ref[0:1, :]
    h_ref[0:1, :] = hs[T - 1:T, :]
    o_ref[0] = (hs * _gelu_tanh(yr_ref[0])).astype(o_ref.dtype)


def _rnn_call(xr, yr, conv_w, conv_b, wg, bg, lam, b_sz, s_len):
    T = min(T_RNN, s_len)
    C = C_RNN
    nc = RNN_WIDTH // C
    x3 = xr.reshape(b_sz, s_len, RNN_WIDTH)
    y3 = yr.reshape(b_sz, s_len, RNN_WIDTH)
    tile = lambda b, c, t: (b, t, c)
    chan = lambda b, c, t: (0, c)
    out = pl.pallas_call(
        _rnn_kernel,
        grid=(b_sz, nc, s_len // T),
        in_specs=[
            pl.BlockSpec((1, T, C), tile),
            pl.BlockSpec((1, T, C), tile),
            pl.BlockSpec((RNN_CONV, C), chan),
            pl.BlockSpec((1, C), chan),
            pl.BlockSpec((1, C, 2 * C), lambda b, c, t: (c, 0, 0)),
            pl.BlockSpec((1, 1, 2 * C), lambda b, c, t: (c, 0, 0)),
            pl.BlockSpec((1, C), chan),
        ],
        out_specs=pl.BlockSpec((1, T, C), tile),
        out_shape=jax.ShapeDtypeStruct((b_sz, s_len, RNN_WIDTH), MXU_DTYPE),
        scratch_shapes=[
            pltpu.VMEM((T + SUBLANES, C), jnp.float32),
            pltpu.VMEM((SUBLANES, C), jnp.float32),
        ],
        compiler_params=_params(("arbitrary", "arbitrary", "arbitrary")),
        name="rglru",
    )(x3, y3, conv_w, conv_b, wg, bg, lam)
    return out.reshape(b_sz * s_len, RNN_WIDTH)


def _moba_kernel(qt_ref, k_ref, vt_ref, kmean_ref, bias_ref, o_ref, sel_ref, qs_ref, sa_ref, sb_ref,
                 pa_ref, pb_ref, ala_ref, alb_ref, m_ref, acc_ref):
    n = pl.program_id(2)
    MB = MOBA_BLOCK
    nb = kmean_ref.shape[1]
    qt = qt_ref[0, 0]
    rows = lax.broadcasted_iota(jnp.int32, (LANES, MB), 0)
    zero = jnp.zeros((), qt.dtype)
    qts = [jnp.where(rows < HEAD_DIM, qt, zero), jnp.where(rows >= HEAD_DIM, qt, zero)]

    kmean = kmean_ref[0].astype(MXU_DTYPE)
    blk = lax.broadcasted_iota(jnp.int32, (nb, MB), 0).astype(jnp.float32)
    n_f = n.astype(jnp.float32)
    past = blk < n_f
    for e in range(2):
        gate = _dot(kmean, qts[e])
        gate = jnp.where(past, gate, NEG_INF)
        sel = jnp.zeros((nb, MB), jnp.float32)
        for _ in range(MOBA_TOPK):
            mx = jnp.max(gate, axis=0, keepdims=True)
            first = jnp.min(jnp.where(gate == mx, blk, float(nb)), axis=0, keepdims=True)
            hit = blk == first
            sel = jnp.where(hit, 1.0, sel)
            gate = jnp.where(hit, TAKEN, gate)
        sel_ref[e] = jnp.where(blk == n_f, 1.0, jnp.where(past, sel, 0.0))

    for e in range(2):
        qs_ref[e] = qts[e]
    m_ref[...] = jnp.full(m_ref.shape, NEG_INF, jnp.float32)
    acc_ref[...] = jnp.zeros(acc_ref.shape, jnp.float32)
    pb_ref[...] = jnp.zeros(pb_ref.shape, pb_ref.dtype)
    alb_ref[...] = jnp.ones(alb_ref.shape, jnp.float32)
    ones_rows = jnp.ones((MOBA_SUM_ROWS, MB), MXU_DTYPE)

    def block_of(t):
        return jnp.clip(n - t, 0, n)

    def accumulate_issue(t, p_ref):
        vt_m = vt_ref[0, block_of(t)]
        pvs = []
        for e in range(2):
            lhs = jnp.concatenate([vt_m[e * HEAD_DIM:(e + 1) * HEAD_DIM, :], ones_rows], axis=0)
            pvs.append(_dot(lhs, p_ref[e]))
        return pvs

    def accumulate_finish(pvs, al_ref):
        for e in range(2):
            acc_ref[e] = al_ref[e] * acc_ref[e] + pvs[e]

    def scores(t, s_ref):
        m = block_of(t)
        k_m = k_ref[0, pl.ds(pl.multiple_of(m * MB, MB), MB), :]
        tile = jnp.minimum(t, MOBA_NEAR)
        for e in range(2):
            s_ref[e] = _dot(k_m, qs_ref[e]) + bias_ref[e, tile]

    def softmax(t, s_ref, p_ref, al_ref):
        m = block_of(t)
        live = t <= n
        for e in range(2):
            s = s_ref[e]
            chosen = (sel_ref[e, pl.ds(m, 1), :] > 0.0) & live
            mx = jnp.where(chosen, jnp.max(s, axis=0, keepdims=True), NEG_INF)
            m_old = m_ref[e]
            m_new = jnp.maximum(m_old, mx)
            al_ref[e] = jnp.exp(m_old - m_new)
            p_ref[e] = jnp.exp(s - jnp.where(chosen, m_new, -NEG_INF)).astype(p_ref.dtype)
            m_ref[e] = m_new

    scores(0, sa_ref)

    def body(i, carry):
        t = 2 * i
        pvs = accumulate_issue(t - 1, pb_ref)
        scores(t + 1, sb_ref)
        softmax(t, sa_ref, pa_ref, ala_ref)
        accumulate_finish(pvs, alb_ref)
        pvs = accumulate_issue(t, pa_ref)
        scores(t + 2, sa_ref)
        softmax(t + 1, sb_ref, pb_ref, alb_ref)
        accumulate_finish(pvs, ala_ref)
        return carry

    n_pairs = (n + 2) // 2
    lax.fori_loop(0, n_pairs, body, 0)
    accumulate_finish(accumulate_issue(2 * n_pairs - 1, pb_ref), alb_ref)

    top = acc_ref[0, 0:HEAD_DIM, :] / acc_ref[0, HEAD_DIM:HEAD_DIM + 1, :]
    bot = acc_ref[1, 0:HEAD_DIM, :] / acc_ref[1, HEAD_DIM:HEAD_DIM + 1, :]
    o_ref[0] = jnp.concatenate([top, bot], axis=0).T.astype(o_ref.dtype)


def _moba_call(qct, kc, vct, kmean, bias_c, b_sz, s_len):
    MB = MOBA_BLOCK
    nb = s_len // MB
    k3 = kc.reshape(b_sz, s_len, MOBA_W)
    km3 = kmean.reshape(b_sz, nb, MOBA_W)
    npair = MOBA_W // LANES
    out = pl.pallas_call(
        _moba_kernel,
        grid=(b_sz, npair, nb),
        in_specs=[
            pl.BlockSpec((1, 1, LANES, MB), lambda b, p, n: (b, n, p, 0)),
            pl.BlockSpec((1, s_len, LANES), lambda b, p, n: (b, 0, p)),
            pl.BlockSpec((1, nb, LANES, MB), lambda b, p, n: (b, 0, p, 0)),
            pl.BlockSpec((1, nb, LANES), lambda b, p, n: (b, 0, p)),
            pl.BlockSpec((2, MOBA_NEAR + 1, MB, MB), lambda b, p, n: (p, 0, 0, 0)),
        ],
        out_specs=pl.BlockSpec((1, MB, LANES), lambda b, p, n: (b, n, p)),
        out_shape=jax.ShapeDtypeStruct((b_sz, s_len, MOBA_W), MXU_DTYPE),
        scratch_shapes=[
            pltpu.VMEM((2, nb, MB), jnp.float32),
            pltpu.VMEM((2, LANES, MB), MXU_DTYPE),
            pltpu.VMEM((2, MB, MB), jnp.float32),
            pltpu.VMEM((2, MB, MB), jnp.float32),
            pltpu.VMEM((2, MB, MB), MXU_DTYPE),
            pltpu.VMEM((2, MB, MB), MXU_DTYPE),
            pltpu.VMEM((2, 1, MB), jnp.float32),
            pltpu.VMEM((2, 1, MB), jnp.float32),
            pltpu.VMEM((2, 1, MB), jnp.float32),
            pltpu.VMEM((2, HEAD_DIM + MOBA_SUM_ROWS, MB), jnp.float32),
        ],
        compiler_params=_params(("arbitrary", "arbitrary", "arbitrary")),
        name="moba",
    )(qct, k3, vct, km3, bias_c)
    return out.reshape(b_sz * s_len, MOBA_W)


def _merge_kernel(x_ref, oa_ref, ob_ref, oc_ref, g_ref, mod_ref, wa_ref, wb_ref, wc_ref, wo_ref, o_ref):
    d = D_MODEL
    merged = _sigmoid(g_ref[:, 0:d]) * _dot(oa_ref[...], wa_ref[...])
    merged = merged + _sigmoid(g_ref[:, d:2 * d]) * _dot(ob_ref[...], wb_ref[...])
    merged = merged + _sigmoid(g_ref[:, 2 * d:3 * d]) * _dot(oc_ref[...], wc_ref[...])
    y = _dot(merged.astype(MXU_DTYPE), wo_ref[...])
    gate_m = mod_ref[0, :, 2 * d:3 * d]
    o_ref[...] = x_ref[...] + gate_m * y


def _merge_call(x2, o_a, o_b, o_c, g_logits, mods_l, wa, wb, wc, wo, s_len):
    n, d = x2.shape
    tm = min(TM_MERGE, s_len)
    tpb = s_len // tm
    row = lambda i: (i, 0)
    const2 = lambda i: (0, 0)
    return pl.pallas_call(
        _merge_kernel,
        grid=(n // tm,),
        in_specs=[
            pl.BlockSpec((tm, d), row),
            pl.BlockSpec((tm, SWA_Q), row),
            pl.BlockSpec((tm, RNN_WIDTH), row),
            pl.BlockSpec((tm, MOBA_W), row),
            pl.BlockSpec((tm, 3 * d), row),
            pl.BlockSpec((1, 1, 6 * d), lambda i: (i // tpb, 0, 0)),
            pl.BlockSpec(wa.shape, const2),
            pl.BlockSpec(wb.shape, const2),
            pl.BlockSpec(wc.shape, const2),
            pl.BlockSpec(wo.shape, const2),
        ],
        out_specs=pl.BlockSpec((tm, d), row),
        out_shape=jax.ShapeDtypeStruct((n, d), jnp.float32),
        compiler_params=_params(("arbitrary",)),
        name="merge",
    )(x2, o_a, o_b, o_c, g_logits, mods_l, wa, wb, wc, wo)


def _shift_rows(u, k, halo, row8):
    H = SUBLANES
    r = pltpu.roll(u, k, 0)
    top = jnp.where(row8 < k, pltpu.roll(halo, k, 0), r[0:H, :])
    return jnp.concatenate([top, r[H:, :]], axis=0)


def _ffn_kernel(x_ref, mod_ref, gain_ref, wu_ref, cw_ref, cb_ref, wd_ref, o_ref, h_ref, act_ref, halo_ref,
                *, tiles_per_seq):
    i = pl.program_id(0)
    d = D_MODEL
    tm = x_ref.shape[0]
    H = SUBLANES
    CH = FFN_CHUNK

    @pl.when(i == 0)
    def _():
        halo_ref[...] = jnp.zeros(halo_ref.shape, jnp.float32)

    x = x_ref[...]
    shift = mod_ref[0, :, 3 * d:4 * d]
    scale = mod_ref[0, :, 4 * d:5 * d]
    ms = jnp.mean(x * x, axis=-1, keepdims=True)
    y = x * lax.rsqrt(ms + RMS_EPS) * gain_ref[...]
    h_ref[...] = (y * (1.0 + scale) + shift).astype(h_ref.dtype)

    seq_start = (i % tiles_per_seq) == 0
    row8 = lax.broadcasted_iota(jnp.int32, (H, CH), 0)

    def conv(col):
        cols = slice(col, col + CH)
        u = _dot(h_ref[...], wu_ref[:, cols])
        halo = jnp.where(seq_start, 0.0, halo_ref[:, cols])
        halo_ref[:, cols] = u[tm - H:tm, :]
        out = u * cw_ref[FFN_CONV - 1:FFN_CONV, cols] + cb_ref[:, cols]
        for k in range(1, FFN_CONV):
            out = out + _shift_rows(u, k, halo, row8) * cw_ref[FFN_CONV - 1 - k:FFN_CONV - k, cols]
        return out

    for c in range(D_FF // CH):
        ug = conv(c * CH)
        uv = conv(D_FF + c * CH)
        act_ref[:, c * CH:(c + 1) * CH] = (ug * _sigmoid(ug) * uv).astype(act_ref.dtype)

    gate_f = mod_ref[0, :, 5 * d:6 * d]
    o_ref[...] = x_ref[...] + gate_f * _dot(act_ref[...], wd_ref[...])


def _ffn_call(x2, mods_l, gain, w_up, conv_w, conv_b, w_down, s_len):
    n, d = x2.shape
    tm = min(TM_FFN, s_len)
    tpb = s_len // tm
    row = lambda i: (i, 0)
    const2 = lambda i: (0, 0)
    kern = functools.partial(_ffn_kernel, tiles_per_seq=tpb)
    return pl.pallas_call(
        kern,
        grid=(n // tm,),
        in_specs=[
            pl.BlockSpec((tm, d), row),
            pl.BlockSpec((1, 1, 6 * d), lambda i: (i // tpb, 0, 0)),
            pl.BlockSpec((1, d), const2),
            pl.BlockSpec(w_up.shape, const2, pipeline_mode=pl.Buffered(1)),
            pl.BlockSpec(conv_w.shape, const2),
            pl.BlockSpec(conv_b.shape, const2),
            pl.BlockSpec(w_down.shape, const2, pipeline_mode=pl.Buffered(1)),
        ],
        out_specs=pl.BlockSpec((tm, d), row),
        out_shape=jax.ShapeDtypeStruct((n, d), jnp.float32),
        scratch_shapes=[
            pltpu.VMEM((tm, d), MXU_DTYPE),
            pltpu.VMEM((tm, D_FF), MXU_DTYPE),
            pltpu.VMEM((SUBLANES, 2 * D_FF), jnp.float32),
        ],
        compiler_params=_params(("arbitrary",)),
        name="ffn",
    )(x2, mods_l, gain, w_up, conv_w, conv_b, w_down)


def _swa_head_order():
    order = []
    for c in range(SWA_Q // LANES):
        for g in range(SWA_KV_HEADS):
            order.append(g * SWA_GROUP + c)
    return order


def _bias_tables(rel_bias):
    bias_a = rel_bias[:, :SWA_HEADS]
    bias_c = rel_bias[:, SWA_HEADS:]
    L = SWA_BLOCK
    qi = np.arange(L)[:, None]
    sj = np.arange(2 * L)[None, :]
    bucket_a = _t5_bucket_np(qi + L - sj)
    tab_a = jnp.transpose(bias_a[bucket_a], (2, 0, 1))
    MB = MOBA_BLOCK
    kj = np.arange(MB)[:, None]
    qq = np.arange(MB)[None, :]
    deltas = np.arange(MOBA_NEAR + 1)[:, None, None]
    bucket_c = _t5_bucket_np(deltas * MB + qq[None] - kj[None])
    assert (bucket_c[MOBA_NEAR] == NUM_BUCKETS - 1).all()
    tab_c = jnp.transpose(bias_c[bucket_c], (3, 0, 1, 2))
    causal = jnp.asarray((deltas > 0) | (kj <= qq)[None])
    tab_c = jnp.where(causal[None], tab_c, NEG_INF)
    return tab_a.astype(jnp.float32), tab_c.astype(jnp.float32)


def kernel(x, c, w_mod, b_mod, norm_mix, norm_ffn, w_in, qnorm_a, knorm_a, sinks, rnn_conv_w, rnn_conv_b,
           rnn_gate_a_w, rnn_gate_a_b, rnn_gate_x_w, rnn_gate_x_b, rnn_lambda, qnorm_c, knorm_c, rel_bias,
           w_branch, w_out, w_up, ffn_conv_w, ffn_conv_b, w_down):
    b_sz, s_len, d = x.shape
    depth = w_in.shape[0]
    assert d == D_MODEL and s_len % MOBA_BLOCK == 0 and b_sz <= SUBLANES
    n = b_sz * s_len
    bf = MXU_DTYPE
    f32 = jnp.float32

    c_pad = jnp.zeros((SUBLANES, d), f32).at[:b_sz].set(c)
    mods = _mod_call(c_pad, w_mod, b_mod)[:, :b_sz]
    mods = mods.reshape(depth, b_sz, 1, 6 * d)

    offs = np.cumsum([0, SWA_Q, SWA_KV, SWA_KV, RNN_WIDTH, RNN_WIDTH, MOBA_W, MOBA_W, MOBA_W, 3 * d])
    o_qa, o_ka, o_va, o_xr, o_yr, o_qc, o_kc, o_vc, o_g, o_end = [int(v) for v in offs]
    head_order = _swa_head_order()
    w_in_b = w_in.astype(bf)
    wn_all = jnp.concatenate(
        [w_in_b[:, :, o_qa + h * HEAD_DIM:o_qa + (h + 1) * HEAD_DIM] for h in head_order]
        + [w_in_b[:, :, o_ka:o_qc], w_in_b[:, :, o_kc:o_vc], w_in_b[:, :, o_g:o_end]], axis=-1)
    wt_all = jnp.swapaxes(jnp.concatenate([w_in_b[:, :, o_qc:o_kc], w_in_b[:, :, o_vc:o_g]], axis=-1), 1, 2)
    lane_head = np.arange(LANES) // HEAD_DIM
    ones_blk = jnp.asarray(lane_head[:, None] == lane_head[None, :], bf)

    wa_all = jnp.concatenate(
        [w_branch[:, h * HEAD_DIM:(h + 1) * HEAD_DIM] for h in head_order], axis=1).astype(bf)
    wb_all = w_branch[:, SWA_Q:SWA_Q + RNN_WIDTH].astype(bf)
    wc_all = w_branch[:, SWA_Q + RNN_WIDTH:].astype(bf)
    wo_all = w_out.astype(bf)
    wup_all = w_up.astype(bf)
    wdn_all = w_down.astype(bf)

    per = C_RNN // RNN_BLOCK_WIDTH
    nct = RNN_WIDTH // C_RNN
    eye = jnp.eye(per, dtype=f32)

    def blockdiag(w):
        w5 = w.reshape(depth, nct, per, RNN_BLOCK_WIDTH, RNN_BLOCK_WIDTH)
        full = jnp.einsum('lcpij,pq->lcpiqj', w5, eye)
        return full.reshape(depth, nct, C_RNN, C_RNN)

    wg_all = jnp.concatenate([blockdiag(rnn_gate_a_w), blockdiag(rnn_gate_x_w)], axis=-1).astype(bf)
    bg_all = jnp.concatenate([rnn_gate_a_b.reshape(depth, nct, 1, C_RNN),
                              rnn_gate_x_b.reshape(depth, nct, 1, C_RNN)], axis=-1)

    tab_a, tab_c = _bias_tables(rel_bias)

    x2 = x.reshape(n, d)
    for l in range(depth):
        gq_a = jnp.tile(qnorm_a[l], SWA_Q // HEAD_DIM)[None, :]
        gk_a = jnp.tile(knorm_a[l], SWA_KV // HEAD_DIM)[None, :]
        gk_c = jnp.tile(knorm_c[l], MOBA_W // HEAD_DIM)[None, :]
        gq_c = jnp.tile(qnorm_c[l], MOBA_W // HEAD_DIM)[:, None]
        qa, ka, va, xr, yr, kc, kmean, g_logits, qct, vct = _inproj_call(
            x2, mods[l], norm_mix[l][None, :], wn_all[l], wt_all[l], ones_blk, gq_a, gk_a, gk_c, gq_c,
            b_sz, s_len)
        o_a = _swa_call(qa, ka, va, sinks[l], tab_a, b_sz, s_len)
        o_b = _rnn_call(xr, yr, rnn_conv_w[l], rnn_conv_b[l][None, :], wg_all[l], bg_all[l],
                        rnn_lambda[l][None, :], b_sz, s_len)
        o_c = _moba_call(qct, kc, vct, kmean, tab_c, b_sz, s_len)
        x2 = _merge_call(x2, o_a, o_b, o_c, g_logits, mods[l], wa_all[l], wb_all[l], wc_all[l], wo_all[l],
                         s_len)
        x2 = _ffn_call(x2, mods[l], norm_ffn[l][None, :], wup_all[l], ffn_conv_w[l], ffn_conv_b[l][None, :],
                       wdn_all[l], s_len)
    return x2.reshape(b_sz, s_len, d)
```

```python
import functools
import math

import numpy as np
import jax
import jax.numpy as jnp
from jax import lax
from jax.experimental import pallas as pl
from jax.experimental.pallas import tpu as pltpu

D_MODEL = 1024
DEPTH = 4
HEAD_DIM = 64
SWA_HEADS = 8
SWA_KV_HEADS = 2
SWA_WINDOW = 128
SWA_BLOCK = 128
RNN_WIDTH = D_MODEL
RNN_BLOCKS = 16
RNN_BLOCK_WIDTH = RNN_WIDTH // RNN_BLOCKS
RNN_CONV = 4
RGLRU_C = 8.0
MOBA_HEADS = 8
MOBA_BLOCK = 256
MOBA_TOPK = 3
NUM_BUCKETS = 32
MAX_DISTANCE = 2048
D_FF = 2816
FFN_CONV = 3
RMS_EPS = 1e-6
NEG_INF = -1e30
TAKEN = -3e38
SWA_Q = SWA_HEADS * HEAD_DIM
SWA_KV = SWA_KV_HEADS * HEAD_DIM
MOBA_W = MOBA_HEADS * HEAD_DIM
SWA_GROUP = SWA_HEADS // SWA_KV_HEADS

LANES = 128
SUBLANES = 8
VMEM_LIMIT_BYTES = 56 * 1024 * 1024

MXU_DTYPE = jnp.bfloat16
SM_SCALE = HEAD_DIM ** -0.5

TM_IN = 512
TM_MERGE = 512
TM_FFN = 512
FFN_CHUNK = 256
T_RNN = 256
C_RNN = 256


def _params(semantics):
    return pltpu.CompilerParams(dimension_semantics=semantics, vmem_limit_bytes=VMEM_LIMIT_BYTES)


def _dot(a, b):
    return jnp.dot(a, b, preferred_element_type=jnp.float32)


def _dot_nt(a, b):
    return lax.dot_general(a, b, (((1,), (1,)), ((), ())), preferred_element_type=jnp.float32)


def _sigmoid(x):
    return 1.0 / (1.0 + jnp.exp(-x))


def _shift_rows(u, k, halo, row8):
    H = SUBLANES
    r = pltpu.roll(u, k, 0)
    top = jnp.where(row8 < k, pltpu.roll(halo, k, 0), r[0:H, :])
    return jnp.concatenate([top, r[H:, :]], axis=0)


def _t5_bucket_np(dist):
    dist = np.maximum(dist, 0)
    max_exact = NUM_BUCKETS // 2
    ratio = np.maximum(dist, 1).astype(np.float32) / np.float32(max_exact)
    log_ratio = np.log(ratio).astype(np.float32) / np.float32(math.log(MAX_DISTANCE / max_exact))
    large = max_exact + (log_ratio * np.float32(NUM_BUCKETS - max_exact)).astype(np.int32)
    large = np.minimum(large, NUM_BUCKETS - 1)
    return np.where(dist < max_exact, dist, large).astype(np.int32)


def _moba_near_blocks():
    delta = 0
    while True:
        lo = max(delta * MOBA_BLOCK - (MOBA_BLOCK - 1), 0)
        if int(_t5_bucket_np(np.array([lo]))[0]) == NUM_BUCKETS - 1:
            return delta
        delta += 1


MOBA_NEAR = _moba_near_blocks()
MOBA_SUM_ROWS = 2 * SUBLANES
MOBA_UNROLL = 4
assert MOBA_NEAR <= 2 * MOBA_UNROLL


def _bias_tile_kernel(tab_ref, bucket_ref, o_ref, *, head0):
    h = pl.program_id(0) + head0
    bucket = bucket_ref[0]
    out = jnp.full(bucket.shape, NEG_INF, jnp.float32)
    for b in range(NUM_BUCKETS):
        out = jnp.where(bucket == b, tab_ref[b, h], out)
    o_ref[0, 0] = out


def _bias_tiles_call(rel_bias, bucket_np, head0, n_heads):
    n_tiles, r, c = bucket_np.shape
    return pl.pallas_call(
        functools.partial(_bias_tile_kernel, head0=head0),
        grid=(n_heads, n_tiles),
        in_specs=[
            pl.BlockSpec(memory_space=pltpu.SMEM),
            pl.BlockSpec((1, r, c), lambda h, t: (t, 0, 0)),
        ],
        out_specs=pl.BlockSpec((1, 1, r, c), lambda h, t: (h, t, 0, 0)),
        out_shape=jax.ShapeDtypeStruct((n_heads, n_tiles, r, c), jnp.float32),
        compiler_params=_params(("arbitrary", "arbitrary")),
        name="bias_tiles",
    )(rel_bias, jnp.asarray(bucket_np, jnp.int32))


def _mod_kernel(c_ref, w_ref, b_ref, o_ref):
    c = c_ref[...]
    c_act = (c * _sigmoid(c)).astype(MXU_DTYPE)
    o_ref[0] = _dot(c_act, w_ref[0].astype(MXU_DTYPE)) + b_ref[0]


def _mod_call(c_pad, w_mod, b_mod):
    depth, d, d6 = w_mod.shape
    tn = 1536
    return pl.pallas_call(
        _mod_kernel,
        grid=(depth, d6 // tn),
        in_specs=[
            pl.BlockSpec((SUBLANES, d), lambda l, j: (0, 0)),
            pl.BlockSpec((1, d, tn), lambda l, j: (l, 0, j)),
            pl.BlockSpec((1, 1, tn), lambda l, j: (l, 0, j)),
        ],
        out_specs=pl.BlockSpec((1, SUBLANES, tn), lambda l, j: (l, 0, j)),
        out_shape=jax.ShapeDtypeStruct((depth, SUBLANES, d6), jnp.float32),
        compiler_params=_params(("arbitrary", "arbitrary")),
        name="adaln_mod",
    )(c_pad, w_mod, b_mod.reshape(depth, 1, d6))


_C_QA = 0
_C_KVA = _C_QA + SWA_Q
_C_XR = _C_KVA + 2 * SWA_KV
_C_YR = _C_XR + RNN_WIDTH
_C_KC = _C_YR + RNN_WIDTH
_C_G = _C_KC + MOBA_W
_C_END = _C_G + 3 * D_MODEL


def _head_norm_rows(z, ones_blk, gain):
    zz = z * z
    hi = zz.astype(jnp.bfloat16)
    lo = (zz - hi.astype(jnp.float32)).astype(jnp.bfloat16)
    outs = []
    for c in range(z.shape[1] // LANES):
        sl = slice(c * LANES, (c + 1) * LANES)
        ss = _dot(hi[:, sl], ones_blk) + _dot(lo[:, sl], ones_blk)
        outs.append(z[:, sl] * lax.rsqrt(ss * (1.0 / HEAD_DIM) + RMS_EPS))
    y = outs[0] if len(outs) == 1 else jnp.concatenate(outs, axis=1)
    return y * gain


def _head_norm_cols(zt, gain_col):
    w, m = zt.shape
    z3 = zt.reshape(w // HEAD_DIM, HEAD_DIM, m)
    ss = jnp.sum(z3 * z3, axis=1, keepdims=True)
    y = z3 * lax.rsqrt(ss * (1.0 / HEAD_DIM) + RMS_EPS)
    return y.reshape(w, m) * gain_col


def _inproj_kernel(x_ref, mod_ref, gain_ref, wn_ref, wt_ref, ones_ref, gq_a_ref, gk_a_ref, gk_c_ref,
                   gq_c_ref, qa_ref, ka_ref, va_ref, xr_ref, yr_ref, kc_ref, kmean_ref, g_ref,
                   qct_ref, vct_ref):
    d = D_MODEL
    x = x_ref[...]
    shift = mod_ref[0, :, 0:d]
    scale = mod_ref[0, :, d:2 * d]
    ms = jnp.mean(x * x, axis=-1, keepdims=True)
    y = x * lax.rsqrt(ms + RMS_EPS) * gain_ref[...]
    h = (y * (1.0 + scale) + shift).astype(MXU_DTYPE)
    ones_blk = ones_ref[...]

    z = _dot(h, wn_ref[:, _C_QA:_C_KVA])
    qa_ref[...] = (_head_norm_rows(z, ones_blk, gq_a_ref[...]) * SM_SCALE).astype(qa_ref.dtype)

    z = _dot(h, wn_ref[:, _C_KVA:_C_XR])
    ka_ref[...] = _head_norm_rows(z[:, :SWA_KV], ones_blk, gk_a_ref[...]).astype(ka_ref.dtype)
    va_ref[...] = z[:, SWA_KV:].astype(va_ref.dtype)

    xr_ref[...] = _dot(h, wn_ref[:, _C_XR:_C_YR])
    yr_ref[...] = _dot(h, wn_ref[:, _C_YR:_C_KC])

    z = _dot(h, wn_ref[:, _C_KC:_C_G])
    kc = _head_norm_rows(z, ones_blk, gk_c_ref[...])
    kc_ref[...] = kc.astype(kc_ref.dtype)
    tm = x.shape[0]
    nblk = tm // MOBA_BLOCK
    kmean_ref[0] = jnp.mean(kc.reshape(nblk, MOBA_BLOCK, MOBA_W), axis=1)

    for c in range(3 * D_MODEL // 512):
        g_ref[:, c * 512:(c + 1) * 512] = _dot(h, wn_ref[:, _C_G + c * 512:_C_G + (c + 1) * 512])

    zt = _dot_nt(wt_ref[0:MOBA_W, :], h)
    qct = (_head_norm_cols(zt, gq_c_ref[...]) * SM_SCALE).astype(qct_ref.dtype)
    zt = _dot_nt(wt_ref[MOBA_W:2 * MOBA_W, :], h)
    vct = zt.astype(vct_ref.dtype)
    for r in range(nblk):
        qct_ref[0, r] = qct[:, r * MOBA_BLOCK:(r + 1) * MOBA_BLOCK]
        vct_ref[0, r] = vct[:, r * MOBA_BLOCK:(r + 1) * MOBA_BLOCK]


def _inproj_call(x2, mods_l, gain, wn, wt, ones_blk, gq_a, gk_a, gk_c, gq_c, b_sz, s_len):
    n, d = x2.shape
    tm = min(TM_IN, s_len)
    tpb = s_len // tm
    nblk = tm // MOBA_BLOCK
    nb = s_len // MOBA_BLOCK
    row = lambda i: (i, 0)
    const2 = lambda i: (0, 0)
    bf = MXU_DTYPE
    out_shape = (
        jax.ShapeDtypeStruct((n, SWA_Q), bf),
        jax.ShapeDtypeStruct((n, SWA_KV), bf),
        jax.ShapeDtypeStruct((n, SWA_KV), bf),
        jax.ShapeDtypeStruct((n, RNN_WIDTH), jnp.float32),
        jax.ShapeDtypeStruct((n, RNN_WIDTH), jnp.float32),
        jax.ShapeDtypeStruct((n, MOBA_W), bf),
        jax.ShapeDtypeStruct((n // tm, nblk, MOBA_W), jnp.float32),
        jax.ShapeDtypeStruct((n, 3 * D_MODEL), jnp.float32),
        jax.ShapeDtypeStruct((b_sz, nb, MOBA_W, MOBA_BLOCK), bf),
        jax.ShapeDtypeStruct((b_sz, nb, MOBA_W, MOBA_BLOCK), bf),
    )
    blk4 = pl.BlockSpec((1, nblk, MOBA_W, MOBA_BLOCK), lambda i: (i // tpb, i % tpb, 0, 0))
    out_specs = (
        pl.BlockSpec((tm, SWA_Q), row),
        pl.BlockSpec((tm, SWA_KV), row),
        pl.BlockSpec((tm, SWA_KV), row),
        pl.BlockSpec((tm, RNN_WIDTH), row),
        pl.BlockSpec((tm, RNN_WIDTH), row),
        pl.BlockSpec((tm, MOBA_W), row),
        pl.BlockSpec((1, nblk, MOBA_W), lambda i: (i, 0, 0)),
        pl.BlockSpec((tm, 3 * D_MODEL), row),
        blk4,
        blk4,
    )
    in_specs = [
        pl.BlockSpec((tm, d), row),
        pl.BlockSpec((1, 1, 6 * d), lambda i: (i // tpb, 0, 0)),
        pl.BlockSpec((1, d), const2),
        pl.BlockSpec(wn.shape, const2, pipeline_mode=pl.Buffered(1)),
        pl.BlockSpec(wt.shape, const2, pipeline_mode=pl.Buffered(1)),
        pl.BlockSpec(ones_blk.shape, const2),
        pl.BlockSpec(gq_a.shape, const2),
        pl.BlockSpec(gk_a.shape, const2),
        pl.BlockSpec(gk_c.shape, const2),
        pl.BlockSpec(gq_c.shape, const2),
    ]
    return pl.pallas_call(
        _inproj_kernel,
        grid=(n // tm,),
        in_specs=in_specs,
        out_specs=out_specs,
        out_shape=out_shape,
        compiler_params=_params(("arbitrary",)),
        name="inproj",
    )(x2, mods_l, gain, wn, wt, ones_blk, gq_a, gk_a, gk_c, gq_c)


def _swa_kernel(sink_ref, q_ref, kp_ref, ko_ref, vp_ref, vo_ref, bias_ref, o_ref, s_ref, p_ref):
    n = pl.program_id(1)
    L = SWA_BLOCK
    kcat = jnp.concatenate([kp_ref[0], ko_ref[0]], axis=0)
    vcat = jnp.concatenate([vp_ref[0], vo_ref[0]], axis=0)
    qi = lax.broadcasted_iota(jnp.int32, (L, 2 * L), 0)
    sj = lax.broadcasted_iota(jnp.int32, (L, 2 * L), 1)
    diff = qi + L - sj
    valid = (diff >= 0) & (diff < SWA_WINDOW) & ((n > 0) | (sj >= L))
    lane = lax.broadcasted_iota(jnp.int32, (L, LANES), 1)
    low = lane < HEAD_DIM
    zero = jnp.zeros((), q_ref.dtype)
    for c in range(SWA_Q // LANES):
        q2 = q_ref[0, :, c * LANES:(c + 1) * LANES]
        for g in range(SWA_KV_HEADS):
            head = g * SWA_GROUP + c
            qm = jnp.where(low if g == 0 else ~low, q2, zero)
            s_ref[head] = jnp.where(valid, _dot_nt(qm, kcat) + bias_ref[head], NEG_INF)
    for head in range(SWA_HEADS):
        s = s_ref[head]
        sink = sink_ref[head]
        m = jnp.maximum(jnp.max(s, axis=-1, keepdims=True), sink)
        p = jnp.exp(s - m)
        denom = jnp.sum(p, axis=-1, keepdims=True) + jnp.exp(sink - m)
        p_ref[head] = (p / denom).astype(p_ref.dtype)
    for c in range(SWA_Q // LANES):
        halves = [_dot(p_ref[g * SWA_GROUP + c], vcat) for g in range(SWA_KV_HEADS)]
        o_ref[0, :, c * LANES:(c + 1) * LANES] = jnp.where(low, halves[0], halves[1]).astype(o_ref.dtype)


def _swa_call(qa, ka, va, sinks_l, bias_a, b_sz, s_len):
    L = SWA_BLOCK
    nblk = s_len // L
    q3 = qa.reshape(b_sz, s_len, SWA_Q)
    k3 = ka.reshape(b_sz, s_len, SWA_KV)
    v3 = va.reshape(b_sz, s_len, SWA_KV)
    prev = lambda b, n: (b, jnp.maximum(n - 1, 0), 0)
    own = lambda b, n: (b, n, 0)
    out = pl.pallas_call(
        _swa_kernel,
        grid=(b_sz, nblk),
        in_specs=[
            pl.BlockSpec(memory_space=pltpu.SMEM),
            pl.BlockSpec((1, L, SWA_Q), own),
            pl.BlockSpec((1, L, SWA_KV), prev),
            pl.BlockSpec((1, L, SWA_KV), own),
            pl.BlockSpec((1, L, SWA_KV), prev),
            pl.BlockSpec((1, L, SWA_KV), own),
            pl.BlockSpec((SWA_HEADS, L, 2 * L), lambda b, n: (0, 0, 0)),
        ],
        out_specs=pl.BlockSpec((1, L, SWA_Q), own),
        out_shape=jax.ShapeDtypeStruct((b_sz, s_len, SWA_Q), MXU_DTYPE),
        scratch_shapes=[
            pltpu.VMEM((SWA_HEADS, L, 2 * L), jnp.float32),
            pltpu.VMEM((SWA_HEADS, L, 2 * L), MXU_DTYPE),
        ],
        compiler_params=_params(("arbitrary", "arbitrary")),
        name="swa",
    )(sinks_l, q3, k3, k3, v3, v3, bias_a)
    return out.reshape(b_sz * s_len, SWA_Q)


def _gelu_tanh(x):
    return 0.5 * x * (1.0 + jnp.tanh(math.sqrt(2.0 / math.pi) * (x + 0.044715 * (x * x * x))))


def _rnn_kernel(xr_ref, yr_ref, cw_ref, cb_ref, wg_ref, bg_ref, lam_ref, o_ref, xbuf_ref, h_ref):
    t = pl.program_id(2)
    T = xr_ref.shape[1]
    C = xr_ref.shape[2]
    H = SUBLANES

    @pl.when(t == 0)
    def _():
        xbuf_ref[...] = jnp.zeros(xbuf_ref.shape, jnp.float32)
        h_ref[...] = jnp.zeros(h_ref.shape, jnp.float32)

    x = xr_ref[0]
    halo = xbuf_ref[...]
    row8 = lax.broadcasted_iota(jnp.int32, (H, C), 0)
    xc = x * cw_ref[RNN_CONV - 1:RNN_CONV, :] + cb_ref[...]
    for j in range(1, RNN_CONV):
        xc = xc + _shift_rows(x, j, halo, row8) * cw_ref[RNN_CONV - 1 - j:RNN_CONV - j, :]
    xbuf_ref[...] = x[T - H:T, :]

    gates = _dot(xc.astype(MXU_DTYPE), wg_ref[0]) + bg_ref[0]
    r = _sigmoid(gates[:, :C])
    i = _sigmoid(gates[:, C:])
    nlam = -lam_ref[...]
    softplus = jnp.maximum(nlam, 0.0) + jnp.log1p(jnp.exp(-jnp.abs(nlam)))
    log_a = (-RGLRU_C) * r * softplus
    a = jnp.exp(log_a)
    z2 = 2.0 * log_a
    a2 = a * a
    one = a2 == 1.0
    neg_expm1 = jnp.where(one, -z2, (1.0 - a2) * z2 / jnp.where(one, 1.0, jnp.log(a2)))
    mult = jnp.sqrt(neg_expm1)
    row = lax.broadcasted_iota(jnp.int32, (T, C), 0)
    mult = jnp.where((row == 0) & (t == 0), 1.0, mult)
    u = mult * (i * xc)

    G = T // H
    a = a.reshape(G, H, C)
    u = u.reshape(G, H, C)
    sub = lax.broadcasted_iota(jnp.int32, (G, H, C), 1)
    step = 1
    while step < H:
        keep = sub >= step
        a_prev = jnp.where(keep, pltpu.roll(a, step, 1), 1.0)
        u_prev = jnp.where(keep, pltpu.roll(u, step, 1), 0.0)
        u = a * u_prev + u
        a = a * a_prev
        step *= 2
    carry = h_ref[0:1, :]
    groups = []
    for g in range(G):
        hg = u[g] + a[g] * carry
        groups.append(hg)
        carry = hg[H - 1:H, :]
    h_ref[0:1, :] = carry
    hs = jnp.concatenate(groups, axis=0)
    o_ref[0] = (hs * _gelu_tanh(yr_ref[0])).astype(o_ref.dtype)


def _rnn_call(xr, yr, conv_w, conv_b, wg, bg, lam, b_sz, s_len):
    T = min(T_RNN, s_len)
    C = C_RNN
    nc = RNN_WIDTH // C
    x3 = xr.reshape(b_sz, s_len, RNN_WIDTH)
    y3 = yr.reshape(b_sz, s_len, RNN_WIDTH)
    tile = lambda b, c, t: (b, t, c)
    chan = lambda b, c, t: (0, c)
    out = pl.pallas_call(
        _rnn_kernel,
        grid=(b_sz, nc, s_len // T),
        in_specs=[
            pl.BlockSpec((1, T, C), tile),
            pl.BlockSpec((1, T, C), tile),
            pl.BlockSpec((RNN_CONV, C), chan),
            pl.BlockSpec((1, C), chan),
            pl.BlockSpec((1, C, 2 * C), lambda b, c, t: (c, 0, 0)),
            pl.BlockSpec((1, 1, 2 * C), lambda b, c, t: (c, 0, 0)),
            pl.BlockSpec((1, C), chan),
        ],
        out_specs=pl.BlockSpec((1, T, C), tile),
        out_shape=jax.ShapeDtypeStruct((b_sz, s_len, RNN_WIDTH), MXU_DTYPE),
        scratch_shapes=[
            pltpu.VMEM((SUBLANES, C), jnp.float32),
            pltpu.VMEM((SUBLANES, C), jnp.float32),
        ],
        compiler_params=_params(("arbitrary", "arbitrary", "arbitrary")),
        name="rglru",
    )(x3, y3, conv_w, conv_b, wg, bg, lam)
    return out.reshape(b_sz * s_len, RNN_WIDTH)


def _moba_kernel(far_ref, qt_ref, k_ref, vt_ref, kmean_ref, bias_ref, o_ref, sel_ref, qs_ref, sa_ref, sb_ref,
                 pa_ref, pb_ref, ala_ref, alb_ref, m_ref, acc_ref):
    n = pl.program_id(2)
    MB = MOBA_BLOCK
    nb = kmean_ref.shape[1]
    qt = qt_ref[0, 0]
    rows = lax.broadcasted_iota(jnp.int32, (LANES, MB), 0)
    zero = jnp.zeros((), qt.dtype)
    qts = [jnp.where(rows < HEAD_DIM, qt, zero), jnp.where(rows >= HEAD_DIM, qt, zero)]

    kmean = kmean_ref[0].astype(MXU_DTYPE)
    blk = lax.broadcasted_iota(jnp.int32, (nb, MB), 0).astype(jnp.float32)
    n_f = n.astype(jnp.float32)
    past = blk < n_f
    for e in range(2):
        gate = _dot(kmean, qts[e])
        gate = jnp.where(past, gate, NEG_INF)
        sel = jnp.zeros((nb, MB), jnp.float32)
        for _ in range(MOBA_TOPK):
            mx = jnp.max(gate, axis=0, keepdims=True)
            first = jnp.min(jnp.where(gate == mx, blk, float(nb)), axis=0, keepdims=True)
            hit = blk == first
            sel = jnp.where(hit, 1.0, sel)
            gate = jnp.where(hit, TAKEN, gate)
        sel_ref[e] = jnp.where(blk == n_f, 1.0, jnp.where(past, sel, 0.0))

    for e in range(2):
        qs_ref[e] = qts[e]
    m_ref[...] = jnp.full(m_ref.shape, NEG_INF, jnp.float32)
    acc_ref[...] = jnp.zeros(acc_ref.shape, jnp.float32)
    pb_ref[...] = jnp.zeros(pb_ref.shape, pb_ref.dtype)
    alb_ref[...] = jnp.ones(alb_ref.shape, jnp.float32)
    ones_rows = jnp.ones((MOBA_SUM_ROWS, MB), MXU_DTYPE)

    def block_of(t):
        return jnp.clip(n - t, 0, n)

    def accumulate_issue(t, p_ref):
        vt_m = vt_ref[0, block_of(t)]
        pvs = []
        for e in range(2):
            lhs = jnp.concatenate([vt_m[e * HEAD_DIM:(e + 1) * HEAD_DIM, :], ones_rows], axis=0)
            pvs.append(_dot(lhs, p_ref[e]))
        return pvs

    def accumulate_finish(pvs, al_ref):
        for e in range(2):
            acc_ref[e] = al_ref[e] * acc_ref[e] + pvs[e]

    far_bias = [far_ref[2 * pl.program_id(1) + e] for e in range(2)]

    def scores(t, s_ref, far):
        m = block_of(t)
        k_m = k_ref[0, pl.ds(pl.multiple_of(m * MB, MB), MB), :]
        for e in range(2):
            s = _dot(k_m, qs_ref[e])
            s_ref[e] = s if far else s + bias_ref[e, t]

    def softmax(t, s_ref, p_ref, al_ref, far):
        m = block_of(t)
        live = t <= n
        for e in range(2):
            s = s_ref[e]
            chosen = (sel_ref[e, pl.ds(m, 1), :] > 0.0) & live
            mx = jnp.max(s, axis=0, keepdims=True)
            if far:
                mx = mx + far_bias[e]
            mx = jnp.where(chosen, mx, NEG_INF)
            m_old = m_ref[e]
            m_new = jnp.maximum(m_old, mx)
            al_ref[e] = jnp.exp(m_old - m_new)
            sub = m_new - far_bias[e] if far else m_new
            p_ref[e] = jnp.exp(s - jnp.where(chosen, sub, -NEG_INF)).astype(p_ref.dtype)
            m_ref[e] = m_new

    bufs = ((sa_ref, pa_ref, ala_ref), (sb_ref, pb_ref, alb_ref))

    def region(t, parity, far, far_next):
        s_cur, p_cur, al_cur = bufs[parity]
        s_oth, p_oth, al_oth = bufs[1 - parity]
        pvs = accumulate_issue(t - 1, p_oth)
        scores(t + 1, s_oth, far_next)
        accumulate_finish(pvs, al_oth)
        softmax(t, s_cur, p_cur, al_cur, far)

    def near_regions(t_lo, t_hi):
        for t in range(t_lo, t_hi):
            region(t, t % 2, t >= MOBA_NEAR, t + 1 >= MOBA_NEAR)

    near_steps = 2 * MOBA_UNROLL
    scores(0, sa_ref, False)
    near_regions(0, MOBA_UNROLL)

    @pl.when(n >= MOBA_UNROLL)
    def _():
        near_regions(MOBA_UNROLL, near_steps)

    def far_regions(t0, count):
        for r in range(count):
            region(t0 + r, r % 2, True, True)

    n_far = jnp.maximum(n + 1 - near_steps, 0)
    n_quads = n_far // MOBA_UNROLL
    n_pairs = (n_far - MOBA_UNROLL * n_quads + 1) // 2

    def quad_body(i, carry):
        far_regions(near_steps + MOBA_UNROLL * i, MOBA_UNROLL)
        return carry

    lax.fori_loop(0, n_quads, quad_body, 0)
    t_pairs = near_steps + MOBA_UNROLL * n_quads

    def pair_body(i, carry):
        far_regions(t_pairs + 2 * i, 2)
        return carry

    lax.fori_loop(0, n_pairs, pair_body, 0)
    t_last = jnp.where(n < MOBA_UNROLL, MOBA_UNROLL - 1, t_pairs + 2 * n_pairs - 1)
    accumulate_finish(accumulate_issue(t_last, pb_ref), alb_ref)

    top = acc_ref[0, 0:HEAD_DIM, :] / acc_ref[0, HEAD_DIM:HEAD_DIM + 1, :]
    bot = acc_ref[1, 0:HEAD_DIM, :] / acc_ref[1, HEAD_DIM:HEAD_DIM + 1, :]
    o_ref[0] = jnp.concatenate([top, bot], axis=0).T.astype(o_ref.dtype)


def _moba_call(qct, kc, vct, kmean, bias_c, far_c, b_sz, s_len):
    MB = MOBA_BLOCK
    nb = s_len // MB
    k3 = kc.reshape(b_sz, s_len, MOBA_W)
    km3 = kmean.reshape(b_sz, nb, MOBA_W)
    npair = MOBA_W // LANES
    out = pl.pallas_call(
        _moba_kernel,
        grid=(b_sz, npair, nb),
        in_specs=[
            pl.BlockSpec(memory_space=pltpu.SMEM),
            pl.BlockSpec((1, 1, LANES, MB), lambda b, p, n: (b, n, p, 0)),
            pl.BlockSpec((1, s_len, LANES), lambda b, p, n: (b, 0, p)),
            pl.BlockSpec((1, nb, LANES, MB), lambda b, p, n: (b, 0, p, 0)),
            pl.BlockSpec((1, nb, LANES), lambda b, p, n: (b, 0, p)),
            pl.BlockSpec((2, MOBA_NEAR, MB, MB), lambda b, p, n: (p, 0, 0, 0)),
        ],
        out_specs=pl.BlockSpec((1, MB, LANES), lambda b, p, n: (b, n, p)),
        out_shape=jax.ShapeDtypeStruct((b_sz, s_len, MOBA_W), MXU_DTYPE),
        scratch_shapes=[
            pltpu.VMEM((2, nb, MB), jnp.float32),
            pltpu.VMEM((2, LANES, MB), MXU_DTYPE),
            pltpu.VMEM((2, MB, MB), jnp.float32),
            pltpu.VMEM((2, MB, MB), jnp.float32),
            pltpu.VMEM((2, MB, MB), MXU_DTYPE),
            pltpu.VMEM((2, MB, MB), MXU_DTYPE),
            pltpu.VMEM((2, 1, MB), jnp.float32),
            pltpu.VMEM((2, 1, MB), jnp.float32),
            pltpu.VMEM((2, 1, MB), jnp.float32),
            pltpu.VMEM((2, HEAD_DIM + MOBA_SUM_ROWS, MB), jnp.float32),
        ],
        compiler_params=_params(("arbitrary", "arbitrary", "arbitrary")),
        name="moba",
    )(far_c, qct, k3, vct, km3, bias_c)
    return out.reshape(b_sz * s_len, MOBA_W)


def _merge_kernel(x_ref, oa_ref, ob_ref, oc_ref, g_ref, mod_ref, wa_ref, wb_ref, wc_ref, wo_ref, o_ref):
    d = D_MODEL
    merged = _sigmoid(g_ref[:, 0:d]) * _dot(oa_ref[...], wa_ref[...])
    merged = merged + _sigmoid(g_ref[:, d:2 * d]) * _dot(ob_ref[...], wb_ref[...])
    merged = merged + _sigmoid(g_ref[:, 2 * d:3 * d]) * _dot(oc_ref[...], wc_ref[...])
    y = _dot(merged.astype(MXU_DTYPE), wo_ref[...])
    gate_m = mod_ref[0, :, 2 * d:3 * d]
    o_ref[...] = x_ref[...] + gate_m * y


def _merge_call(x2, o_a, o_b, o_c, g_logits, mods_l, wa, wb, wc, wo, s_len):
    n, d = x2.shape
    tm = min(TM_MERGE, s_len)
    tpb = s_len // tm
    row = lambda i: (i, 0)
    const2 = lambda i: (0, 0)
    return pl.pallas_call(
        _merge_kernel,
        grid=(n // tm,),
        in_specs=[
            pl.BlockSpec((tm, d), row),
            pl.BlockSpec((tm, SWA_Q), row),
            pl.BlockSpec((tm, RNN_WIDTH), row),
            pl.BlockSpec((tm, MOBA_W), row),
            pl.BlockSpec((tm, 3 * d), row),
            pl.BlockSpec((1, 1, 6 * d), lambda i: (i // tpb, 0, 0)),
            pl.BlockSpec(wa.shape, const2),
            pl.BlockSpec(wb.shape, const2),
            pl.BlockSpec(wc.shape, const2),
            pl.BlockSpec(wo.shape, const2),
        ],
        out_specs=pl.BlockSpec((tm, d), row),
        out_shape=jax.ShapeDtypeStruct((n, d), jnp.float32),
        compiler_params=_params(("arbitrary",)),
        name="merge",
    )(x2, o_a, o_b, o_c, g_logits, mods_l, wa, wb, wc, wo)


def _ffn_kernel(x_ref, mod_ref, gain_ref, wu_ref, cw_ref, cb_ref, wd_ref, o_ref, h_ref, act_ref, halo_ref,
                *, tiles_per_seq):
    i = pl.program_id(0)
    d = D_MODEL
    tm = x_ref.shape[0]
    H = SUBLANES
    CH = FFN_CHUNK

    @pl.when(i == 0)
    def _():
        halo_ref[...] = jnp.zeros(halo_ref.shape, jnp.float32)

    x = x_ref[...]
    shift = mod_ref[0, :, 3 * d:4 * d]
    scale = mod_ref[0, :, 4 * d:5 * d]
    ms = jnp.mean(x * x, axis=-1, keepdims=True)
    y = x * lax.rsqrt(ms + RMS_EPS) * gain_ref[...]
    h_ref[...] = (y * (1.0 + scale) + shift).astype(h_ref.dtype)

    seq_start = (i % tiles_per_seq) == 0
    row8 = lax.broadcasted_iota(jnp.int32, (H, CH), 0)

    def conv(col):
        cols = slice(col, col + CH)
        u = _dot(h_ref[...], wu_ref[:, cols])
        halo = jnp.where(seq_start, 0.0, halo_ref[:, cols])
        halo_ref[:, cols] = u[tm - H:tm, :]
        out = u * cw_ref[FFN_CONV - 1:FFN_CONV, cols] + cb_ref[:, cols]
        for k in range(1, FFN_CONV):
            out = out + _shift_rows(u, k, halo, row8) * cw_ref[FFN_CONV - 1 - k:FFN_CONV - k, cols]
        return out

    for c in range(D_FF // CH):
        ug = conv(c * CH)
        uv = conv(D_FF + c * CH)
        act_ref[:, c * CH:(c + 1) * CH] = (ug * _sigmoid(ug) * uv).astype(act_ref.dtype)

    gate_f = mod_ref[0, :, 5 * d:6 * d]
    o_ref[...] = x_ref[...] + gate_f * _dot(act_ref[...], wd_ref[...])


def _ffn_call(x2, mods_l, gain, w_up, conv_w, conv_b, w_down, s_len):
    n, d = x2.shape
    tm = min(TM_FFN, s_len)
    tpb = s_len // tm
    row = lambda i: (i, 0)
    const2 = lambda i: (0, 0)
    kern = functools.partial(_ffn_kernel, tiles_per_seq=tpb)
    return pl.pallas_call(
        kern,
        grid=(n // tm,),
        in_specs=[
            pl.BlockSpec((tm, d), row),
            pl.BlockSpec((1, 1, 6 * d), lambda i: (i // tpb, 0, 0)),
            pl.BlockSpec((1, d), const2),
            pl.BlockSpec(w_up.shape, const2, pipeline_mode=pl.Buffered(1)),
            pl.BlockSpec(conv_w.shape, const2),
            pl.BlockSpec(conv_b.shape, const2),
            pl.BlockSpec(w_down.shape, const2, pipeline_mode=pl.Buffered(1)),
        ],
        out_specs=pl.BlockSpec((tm, d), row),
        out_shape=jax.ShapeDtypeStruct((n, d), jnp.float32),
        scratch_shapes=[
            pltpu.VMEM((tm, d), MXU_DTYPE),
            pltpu.VMEM((tm, D_FF), MXU_DTYPE),
            pltpu.VMEM((SUBLANES, 2 * D_FF), jnp.float32),
        ],
        compiler_params=_params(("arbitrary",)),
        name="ffn",
    )(x2, mods_l, gain, w_up, conv_w, conv_b, w_down)


def _swa_head_order():
    order = []
    for c in range(SWA_Q // LANES):
        for g in range(SWA_KV_HEADS):
            order.append(g * SWA_GROUP + c)
    return order


def _bias_tables(rel_bias):
    L = SWA_BLOCK
    qi = np.arange(L)[:, None]
    sj = np.arange(2 * L)[None, :]
    bucket_a = _t5_bucket_np(qi + L - sj)[None]
    tab_a = _bias_tiles_call(rel_bias, bucket_a, 0, SWA_HEADS)[:, 0]
    MB = MOBA_BLOCK
    kj = np.arange(MB)[:, None]
    qq = np.arange(MB)[None, :]
    deltas = np.arange(MOBA_NEAR)[:, None, None]
    bucket_c = _t5_bucket_np(deltas * MB + qq[None] - kj[None])
    bucket_c = np.where((deltas > 0) | (kj <= qq)[None], bucket_c, NUM_BUCKETS)
    tab_c = _bias_tiles_call(rel_bias, bucket_c, SWA_HEADS, MOBA_HEADS)
    far_c = rel_bias[NUM_BUCKETS - 1, SWA_HEADS:]
    return tab_a, tab_c, far_c


def kernel(x, c, w_mod, b_mod, norm_mix, norm_ffn, w_in, qnorm_a, knorm_a, sinks, rnn_conv_w, rnn_conv_b,
           rnn_gate_a_w, rnn_gate_a_b, rnn_gate_x_w, rnn_gate_x_b, rnn_lambda, qnorm_c, knorm_c, rel_bias,
           w_branch, w_out, w_up, ffn_conv_w, ffn_conv_b, w_down):
    b_sz, s_len, d = x.shape
    depth = w_in.shape[0]
    assert d == D_MODEL and s_len % MOBA_BLOCK == 0 and b_sz <= SUBLANES
    n = b_sz * s_len
    bf = MXU_DTYPE
    f32 = jnp.float32

    c_pad = jnp.zeros((SUBLANES, d), f32).at[:b_sz].set(c)
    mods = _mod_call(c_pad, w_mod, b_mod)[:, :b_sz]
    mods = mods.reshape(depth, b_sz, 1, 6 * d)

    offs = np.cumsum([0, SWA_Q, SWA_KV, SWA_KV, RNN_WIDTH, RNN_WIDTH, MOBA_W, MOBA_W, MOBA_W, 3 * d])
    o_qa, o_ka, o_va, o_xr, o_yr, o_qc, o_kc, o_vc, o_g, o_end = [int(v) for v in offs]
    head_order = _swa_head_order()
    w_in_b = w_in.astype(bf)
    wn_all = jnp.concatenate(
        [w_in_b[:, :, o_qa + h * HEAD_DIM:o_qa + (h + 1) * HEAD_DIM] for h in head_order]
        + [w_in_b[:, :, o_ka:o_qc], w_in_b[:, :, o_kc:o_vc], w_in_b[:, :, o_g:o_end]], axis=-1)
    wt_all = jnp.swapaxes(jnp.concatenate([w_in_b[:, :, o_qc:o_kc], w_in_b[:, :, o_vc:o_g]], axis=-1), 1, 2)
    lane_head = np.arange(LANES) // HEAD_DIM
    ones_blk = jnp.asarray(lane_head[:, None] == lane_head[None, :], bf)

    wa_all = jnp.concatenate(
        [w_branch[:, h * HEAD_DIM:(h + 1) * HEAD_DIM] for h in head_order], axis=1).astype(bf)
    wb_all = w_branch[:, SWA_Q:SWA_Q + RNN_WIDTH].astype(bf)
    wc_all = w_branch[:, SWA_Q + RNN_WIDTH:].astype(bf)
    wo_all = w_out.astype(bf)
    wup_all = w_up.astype(bf)
    wdn_all = w_down.astype(bf)

    per = C_RNN // RNN_BLOCK_WIDTH
    nct = RNN_WIDTH // C_RNN
    eye = jnp.eye(per, dtype=f32)

    def blockdiag(w):
        w5 = w.reshape(depth, nct, per, RNN_BLOCK_WIDTH, RNN_BLOCK_WIDTH)
        full = jnp.einsum('lcpij,pq->lcpiqj', w5, eye)
        return full.reshape(depth, nct, C_RNN, C_RNN)

    wg_all = jnp.concatenate([blockdiag(rnn_gate_a_w), blockdiag(rnn_gate_x_w)], axis=-1).astype(bf)
    bg_all = jnp.concatenate([rnn_gate_a_b.reshape(depth, nct, 1, C_RNN),
                              rnn_gate_x_b.reshape(depth, nct, 1, C_RNN)], axis=-1)

    tab_a, tab_c, far_c = _bias_tables(rel_bias)

    x2 = x.reshape(n, d)
    for l in range(depth):
        gq_a = jnp.tile(qnorm_a[l], SWA_Q // HEAD_DIM)[None, :]
        gk_a = jnp.tile(knorm_a[l], SWA_KV // HEAD_DIM)[None, :]
        gk_c = jnp.tile(knorm_c[l], MOBA_W // HEAD_DIM)[None, :]
        gq_c = jnp.tile(qnorm_c[l], MOBA_W // HEAD_DIM)[:, None]
        qa, ka, va, xr, yr, kc, kmean, g_logits, qct, vct = _inproj_call(
            x2, mods[l], norm_mix[l][None, :], wn_all[l], wt_all[l], ones_blk, gq_a, gk_a, gk_c, gq_c,
            b_sz, s_len)
        o_a = _swa_call(qa, ka, va, sinks[l], tab_a, b_sz, s_len)
        o_b = _rnn_call(xr, yr, rnn_conv_w[l], rnn_conv_b[l][None, :], wg_all[l], bg_all[l],
                        rnn_lambda[l][None, :], b_sz, s_len)
        o_c = _moba_call(qct, kc, vct, kmean, tab_c, far_c, b_sz, s_len)
        x2 = _merge_call(x2, o_a, o_b, o_c, g_logits, mods[l], wa_all[l], wb_all[l], wc_all[l], wo_all[l],
                         s_len)
        x2 = _ffn_call(x2, mods[l], norm_ffn[l][None, :], wup_all[l], ffn_conv_w[l], ffn_conv_b[l][None, :],
                       wdn_all[l], s_len)
    return x2.reshape(b_sz, s_len, d)
```

```python
import functools
import math

import numpy as np
import jax
import jax.numpy as jnp
from jax import lax
from jax.experimental import pallas as pl
from jax.experimental.pallas import tpu as pltpu

D_MODEL = 1024
DEPTH = 4
HEAD_DIM = 64
SWA_HEADS = 8
SWA_KV_HEADS = 2
SWA_WINDOW = 128
SWA_BLOCK = 128
RNN_WIDTH = D_MODEL
RNN_BLOCKS = 16
RNN_BLOCK_WIDTH = RNN_WIDTH // RNN_BLOCKS
RNN_CONV = 4
RGLRU_C = 8.0
MOBA_HEADS = 8
MOBA_BLOCK = 256
MOBA_TOPK = 3
NUM_BUCKETS = 32
MAX_DISTANCE = 2048
D_FF = 2816
FFN_CONV = 3
RMS_EPS = 1e-6
NEG_INF = -1e30
TAKEN = -3e38
SWA_Q = SWA_HEADS * HEAD_DIM
SWA_KV = SWA_KV_HEADS * HEAD_DIM
MOBA_W = MOBA_HEADS * HEAD_DIM
SWA_GROUP = SWA_HEADS // SWA_KV_HEADS

LANES = 128
SUBLANES = 8
VMEM_LIMIT_BYTES = 56 * 1024 * 1024

MXU_DTYPE = jnp.bfloat16
SM_SCALE = HEAD_DIM ** -0.5

TM_IN = 512
TM_MERGE = 512
TM_FFN = 512
FFN_CHUNK = 256
T_RNN = 256
C_RNN = 256


def _params(semantics):
    return pltpu.CompilerParams(dimension_semantics=semantics, vmem_limit_bytes=VMEM_LIMIT_BYTES)


def _dot(a, b):
    return jnp.dot(a, b, preferred_element_type=jnp.float32)


def _dot_nt(a, b):
    return lax.dot_general(a, b, (((1,), (1,)), ((), ())), preferred_element_type=jnp.float32)


def _sigmoid(x):
    return 1.0 / (1.0 + jnp.exp(-x))


def _shift_rows(u, k, halo, row8):
    H = SUBLANES
    r = pltpu.roll(u, k, 0)
    top = jnp.where(row8 < k, pltpu.roll(halo, k, 0), r[0:H, :])
    return jnp.concatenate([top, r[H:, :]], axis=0)


def _t5_bucket_np(dist):
    dist = np.maximum(dist, 0)
    max_exact = NUM_BUCKETS // 2
    ratio = np.maximum(dist, 1).astype(np.float32) / np.float32(max_exact)
    log_ratio = np.log(ratio).astype(np.float32) / np.float32(math.log(MAX_DISTANCE / max_exact))
    large = max_exact + (log_ratio * np.float32(NUM_BUCKETS - max_exact)).astype(np.int32)
    large = np.minimum(large, NUM_BUCKETS - 1)
    return np.where(dist < max_exact, dist, large).astype(np.int32)


def _moba_near_blocks():
    delta = 0
    while True:
        lo = max(delta * MOBA_BLOCK - (MOBA_BLOCK - 1), 0)
        if int(_t5_bucket_np(np.array([lo]))[0]) == NUM_BUCKETS - 1:
            return delta
        delta += 1


MOBA_NEAR = _moba_near_blocks()
MOBA_SUM_ROWS = 2 * SUBLANES
MOBA_UNROLL = 4
MOBA_AHEAD = 2
assert MOBA_NEAR <= 2 * MOBA_UNROLL and MOBA_UNROLL == 4 and MOBA_AHEAD < MOBA_UNROLL


def _bias_tile_kernel(tab_ref, bucket_ref, o_ref, *, head0):
    h = pl.program_id(0) + head0
    bucket = bucket_ref[0]
    out = jnp.full(bucket.shape, NEG_INF, jnp.float32)
    for b in range(NUM_BUCKETS):
        out = jnp.where(bucket == b, tab_ref[b, h], out)
    o_ref[0, 0] = out


def _bias_tiles_call(rel_bias, bucket_np, head0, n_heads):
    n_tiles, r, c = bucket_np.shape
    return pl.pallas_call(
        functools.partial(_bias_tile_kernel, head0=head0),
        grid=(n_heads, n_tiles),
        in_specs=[
            pl.BlockSpec(memory_space=pltpu.SMEM),
            pl.BlockSpec((1, r, c), lambda h, t: (t, 0, 0)),
        ],
        out_specs=pl.BlockSpec((1, 1, r, c), lambda h, t: (h, t, 0, 0)),
        out_shape=jax.ShapeDtypeStruct((n_heads, n_tiles, r, c), jnp.float32),
        compiler_params=_params(("arbitrary", "arbitrary")),
        name="bias_tiles",
    )(rel_bias, jnp.asarray(bucket_np, jnp.int32))


def _mod_kernel(c_ref, w_ref, b_ref, o_ref):
    c = c_ref[...]
    c_act = (c * _sigmoid(c)).astype(MXU_DTYPE)
    o_ref[0] = _dot(c_act, w_ref[0].astype(MXU_DTYPE)) + b_ref[0]


def _mod_call(c_pad, w_mod, b_mod):
    depth, d, d6 = w_mod.shape
    tn = 1536
    return pl.pallas_call(
        _mod_kernel,
        grid=(depth, d6 // tn),
        in_specs=[
            pl.BlockSpec((SUBLANES, d), lambda l, j: (0, 0)),
            pl.BlockSpec((1, d, tn), lambda l, j: (l, 0, j)),
            pl.BlockSpec((1, 1, tn), lambda l, j: (l, 0, j)),
        ],
        out_specs=pl.BlockSpec((1, SUBLANES, tn), lambda l, j: (l, 0, j)),
        out_shape=jax.ShapeDtypeStruct((depth, SUBLANES, d6), jnp.float32),
        compiler_params=_params(("arbitrary", "arbitrary")),
        name="adaln_mod",
    )(c_pad, w_mod, b_mod.reshape(depth, 1, d6))


_C_QA = 0
_C_KVA = _C_QA + SWA_Q
_C_XR = _C_KVA + 2 * SWA_KV
_C_YR = _C_XR + RNN_WIDTH
_C_KC = _C_YR + RNN_WIDTH
_C_G = _C_KC + MOBA_W
_C_END = _C_G + 3 * D_MODEL


def _head_norm_rows(z, ones_blk, gain):
    zz = z * z
    hi = zz.astype(jnp.bfloat16)
    lo = (zz - hi.astype(jnp.float32)).astype(jnp.bfloat16)
    outs = []
    for c in range(z.shape[1] // LANES):
        sl = slice(c * LANES, (c + 1) * LANES)
        ss = _dot(hi[:, sl], ones_blk) + _dot(lo[:, sl], ones_blk)
        outs.append(z[:, sl] * lax.rsqrt(ss * (1.0 / HEAD_DIM) + RMS_EPS))
    y = outs[0] if len(outs) == 1 else jnp.concatenate(outs, axis=1)
    return y * gain


def _head_norm_cols(zt, gain_col):
    w, m = zt.shape
    z3 = zt.reshape(w // HEAD_DIM, HEAD_DIM, m)
    ss = jnp.sum(z3 * z3, axis=1, keepdims=True)
    y = z3 * lax.rsqrt(ss * (1.0 / HEAD_DIM) + RMS_EPS)
    return y.reshape(w, m) * gain_col


def _inproj_kernel(x_ref, mod_ref, gain_ref, wn_ref, wt_ref, ones_ref, gq_a_ref, gk_a_ref, gk_c_ref,
                   gq_c_ref, qa_ref, ka_ref, va_ref, xr_ref, yr_ref, kc_ref, kmean_ref, g_ref,
                   qct_ref, vct_ref):
    d = D_MODEL
    x = x_ref[...]
    shift = mod_ref[0, :, 0:d]
    scale = mod_ref[0, :, d:2 * d]
    ms = jnp.mean(x * x, axis=-1, keepdims=True)
    y = x * lax.rsqrt(ms + RMS_EPS) * gain_ref[...]
    h = (y * (1.0 + scale) + shift).astype(MXU_DTYPE)
    ones_blk = ones_ref[...]

    z = _dot(h, wn_ref[:, _C_QA:_C_KVA])
    qa_ref[...] = (_head_norm_rows(z, ones_blk, gq_a_ref[...]) * SM_SCALE).astype(qa_ref.dtype)

    z = _dot(h, wn_ref[:, _C_KVA:_C_XR])
    ka_ref[...] = _head_norm_rows(z[:, :SWA_KV], ones_blk, gk_a_ref[...]).astype(ka_ref.dtype)
    va_ref[...] = z[:, SWA_KV:].astype(va_ref.dtype)

    xr_ref[...] = _dot(h, wn_ref[:, _C_XR:_C_YR])
    yr_ref[...] = _dot(h, wn_ref[:, _C_YR:_C_KC])

    z = _dot(h, wn_ref[:, _C_KC:_C_G])
    kc = _head_norm_rows(z, ones_blk, gk_c_ref[...])
    kc_ref[...] = kc.astype(kc_ref.dtype)
    tm = x.shape[0]
    nblk = tm // MOBA_BLOCK
    kmean_ref[0] = jnp.mean(kc.reshape(nblk, MOBA_BLOCK, MOBA_W), axis=1)

    for c in range(3 * D_MODEL // 512):
        g_ref[:, c * 512:(c + 1) * 512] = _dot(h, wn_ref[:, _C_G + c * 512:_C_G + (c + 1) * 512])

    zt = _dot_nt(wt_ref[0:MOBA_W, :], h)
    qct = (_head_norm_cols(zt, gq_c_ref[...]) * SM_SCALE).astype(qct_ref.dtype)
    zt = _dot_nt(wt_ref[MOBA_W:2 * MOBA_W, :], h)
    vct = zt.astype(vct_ref.dtype)
    for r in range(nblk):
        qct_ref[0, r] = qct[:, r * MOBA_BLOCK:(r + 1) * MOBA_BLOCK]
        vct_ref[0, r] = vct[:, r * MOBA_BLOCK:(r + 1) * MOBA_BLOCK]


def _inproj_call(x2, mods_l, gain, wn, wt, ones_blk, gq_a, gk_a, gk_c, gq_c, b_sz, s_len):
    n, d = x2.shape
    tm = min(TM_IN, s_len)
    tpb = s_len // tm
    nblk = tm // MOBA_BLOCK
    nb = s_len // MOBA_BLOCK
    row = lambda i: (i, 0)
    const2 = lambda i: (0, 0)
    bf = MXU_DTYPE
    out_shape = (
        jax.ShapeDtypeStruct((n, SWA_Q), bf),
        jax.ShapeDtypeStruct((n, SWA_KV), bf),
        jax.ShapeDtypeStruct((n, SWA_KV), bf),
        jax.ShapeDtypeStruct((n, RNN_WIDTH), jnp.float32),
        jax.ShapeDtypeStruct((n, RNN_WIDTH), jnp.float32),
        jax.ShapeDtypeStruct((n, MOBA_W), bf),
        jax.ShapeDtypeStruct((n // tm, nblk, MOBA_W), jnp.float32),
        jax.ShapeDtypeStruct((n, 3 * D_MODEL), jnp.float32),
        jax.ShapeDtypeStruct((b_sz, nb, MOBA_W, MOBA_BLOCK), bf),
        jax.ShapeDtypeStruct((b_sz, nb, MOBA_W, MOBA_BLOCK), bf),
    )
    blk4 = pl.BlockSpec((1, nblk, MOBA_W, MOBA_BLOCK), lambda i: (i // tpb, i % tpb, 0, 0))
    out_specs = (
        pl.BlockSpec((tm, SWA_Q), row),
        pl.BlockSpec((tm, SWA_KV), row),
        pl.BlockSpec((tm, SWA_KV), row),
        pl.BlockSpec((tm, RNN_WIDTH), row),
        pl.BlockSpec((tm, RNN_WIDTH), row),
        pl.BlockSpec((tm, MOBA_W), row),
        pl.BlockSpec((1, nblk, MOBA_W), lambda i: (i, 0, 0)),
        pl.BlockSpec((tm, 3 * D_MODEL), row),
        blk4,
        blk4,
    )
    in_specs = [
        pl.BlockSpec((tm, d), row),
        pl.BlockSpec((1, 1, 6 * d), lambda i: (i // tpb, 0, 0)),
        pl.BlockSpec((1, d), const2),
        pl.BlockSpec(wn.shape, const2, pipeline_mode=pl.Buffered(1)),
        pl.BlockSpec(wt.shape, const2, pipeline_mode=pl.Buffered(1)),
        pl.BlockSpec(ones_blk.shape, const2),
        pl.BlockSpec(gq_a.shape, const2),
        pl.BlockSpec(gk_a.shape, const2),
        pl.BlockSpec(gk_c.shape, const2),
        pl.BlockSpec(gq_c.shape, const2),
    ]
    return pl.pallas_call(
        _inproj_kernel,
        grid=(n // tm,),
        in_specs=in_specs,
        out_specs=out_specs,
        out_shape=out_shape,
        compiler_params=_params(("arbitrary",)),
        name="inproj",
    )(x2, mods_l, gain, wn, wt, ones_blk, gq_a, gk_a, gk_c, gq_c)


def _swa_kernel(sink_ref, q_ref, kp_ref, ko_ref, vp_ref, vo_ref, bias_ref, o_ref, s_ref, p_ref):
    n = pl.program_id(1)
    L = SWA_BLOCK
    kcat = jnp.concatenate([kp_ref[0], ko_ref[0]], axis=0)
    vcat = jnp.concatenate([vp_ref[0], vo_ref[0]], axis=0)
    qi = lax.broadcasted_iota(jnp.int32, (L, 2 * L), 0)
    sj = lax.broadcasted_iota(jnp.int32, (L, 2 * L), 1)
    diff = qi + L - sj
    valid = (diff >= 0) & (diff < SWA_WINDOW) & ((n > 0) | (sj >= L))
    lane = lax.broadcasted_iota(jnp.int32, (L, LANES), 1)
    low = lane < HEAD_DIM
    zero = jnp.zeros((), q_ref.dtype)
    for c in range(SWA_Q // LANES):
        q2 = q_ref[0, :, c * LANES:(c + 1) * LANES]
        for g in range(SWA_KV_HEADS):
            head = g * SWA_GROUP + c
            qm = jnp.where(low if g == 0 else ~low, q2, zero)
            s_ref[head] = jnp.where(valid, _dot_nt(qm, kcat) + bias_ref[head], NEG_INF)
    for head in range(SWA_HEADS):
        s = s_ref[head]
        sink = sink_ref[head]
        m = jnp.maximum(jnp.max(s, axis=-1, keepdims=True), sink)
        p = jnp.exp(s - m)
        denom = jnp.sum(p, axis=-1, keepdims=True) + jnp.exp(sink - m)
        p_ref[head] = (p / denom).astype(p_ref.dtype)
    for c in range(SWA_Q // LANES):
        halves = [_dot(p_ref[g * SWA_GROUP + c], vcat) for g in range(SWA_KV_HEADS)]
        o_ref[0, :, c * LANES:(c + 1) * LANES] = jnp.where(low, halves[0], halves[1]).astype(o_ref.dtype)


def _swa_call(qa, ka, va, sinks_l, bias_a, b_sz, s_len):
    L = SWA_BLOCK
    nblk = s_len // L
    q3 = qa.reshape(b_sz, s_len, SWA_Q)
    k3 = ka.reshape(b_sz, s_len, SWA_KV)
    v3 = va.reshape(b_sz, s_len, SWA_KV)
    prev = lambda b, n: (b, jnp.maximum(n - 1, 0), 0)
    own = lambda b, n: (b, n, 0)
    out = pl.pallas_call(
        _swa_kernel,
        grid=(b_sz, nblk),
        in_specs=[
            pl.BlockSpec(memory_space=pltpu.SMEM),
            pl.BlockSpec((1, L, SWA_Q), own),
            pl.BlockSpec((1, L, SWA_KV), prev),
            pl.BlockSpec((1, L, SWA_KV), own),
            pl.BlockSpec((1, L, SWA_KV), prev),
            pl.BlockSpec((1, L, SWA_KV), own),
            pl.BlockSpec((SWA_HEADS, L, 2 * L), lambda b, n: (0, 0, 0)),
        ],
        out_specs=pl.BlockSpec((1, L, SWA_Q), own),
        out_shape=jax.ShapeDtypeStruct((b_sz, s_len, SWA_Q), MXU_DTYPE),
        scratch_shapes=[
            pltpu.VMEM((SWA_HEADS, L, 2 * L), jnp.float32),
            pltpu.VMEM((SWA_HEADS, L, 2 * L), MXU_DTYPE),
        ],
        compiler_params=_params(("arbitrary", "arbitrary")),
        name="swa",
    )(sinks_l, q3, k3, k3, v3, v3, bias_a)
    return out.reshape(b_sz * s_len, SWA_Q)


def _gelu_tanh(x):
    return 0.5 * x * (1.0 + jnp.tanh(math.sqrt(2.0 / math.pi) * (x + 0.044715 * (x * x * x))))


def _rnn_kernel(xr_ref, yr_ref, cw_ref, cb_ref, wg_ref, bg_ref, lam_ref, o_ref, xbuf_ref, h_ref):
    t = pl.program_id(2)
    T = xr_ref.shape[1]
    C = xr_ref.shape[2]
    H = SUBLANES

    @pl.when(t == 0)
    def _():
        xbuf_ref[...] = jnp.zeros(xbuf_ref.shape, jnp.float32)
        h_ref[...] = jnp.zeros(h_ref.shape, jnp.float32)

    x = xr_ref[0]
    halo = xbuf_ref[...]
    row8 = lax.broadcasted_iota(jnp.int32, (H, C), 0)
    xc = x * cw_ref[RNN_CONV - 1:RNN_CONV, :] + cb_ref[...]
    for j in range(1, RNN_CONV):
        xc = xc + _shift_rows(x, j, halo, row8) * cw_ref[RNN_CONV - 1 - j:RNN_CONV - j, :]
    xbuf_ref[...] = x[T - H:T, :]

    gates = _dot(xc.astype(MXU_DTYPE), wg_ref[0]) + bg_ref[0]
    r = _sigmoid(gates[:, :C])
    i = _sigmoid(gates[:, C:])
    nlam = -lam_ref[...]
    softplus = jnp.maximum(nlam, 0.0) + jnp.log1p(jnp.exp(-jnp.abs(nlam)))
    log_a = (-RGLRU_C) * r * softplus
    a = jnp.exp(log_a)
    mult = jnp.sqrt(jnp.tanh(-log_a) * (1.0 + a * a))
    row = lax.broadcasted_iota(jnp.int32, (T, C), 0)
    mult = jnp.where((row == 0) & (t == 0), 1.0, mult)
    u = mult * (i * xc)

    G = T // H
    a = a.reshape(G, H, C)
    u = u.reshape(G, H, C)
    sub = lax.broadcasted_iota(jnp.int32, (G, H, C), 1)
    step = 1
    while step < H:
        keep = sub >= step
        a_prev = jnp.where(keep, pltpu.roll(a, step, 1), 1.0)
        u_prev = jnp.where(keep, pltpu.roll(u, step, 1), 0.0)
        u = a * u_prev + u
        a = a * a_prev
        step *= 2
    carry = h_ref[0:1, :]
    groups = []
    for g in range(G):
        hg = u[g] + a[g] * carry
        groups.append(hg)
        carry = hg[H - 1:H, :]
    h_ref[0:1, :] = carry
    hs = jnp.concatenate(groups, axis=0)
    o_ref[0] = (hs * _gelu_tanh(yr_ref[0])).astype(o_ref.dtype)


def _rnn_call(xr, yr, conv_w, conv_b, wg, bg, lam, b_sz, s_len):
    T = min(T_RNN, s_len)
    C = C_RNN
    nc = RNN_WIDTH // C
    x3 = xr.reshape(b_sz, s_len, RNN_WIDTH)
    y3 = yr.reshape(b_sz, s_len, RNN_WIDTH)
    tile = lambda b, c, t: (b, t, c)
    chan = lambda b, c, t: (0, c)
    out = pl.pallas_call(
        _rnn_kernel,
        grid=(b_sz, nc, s_len // T),
        in_specs=[
            pl.BlockSpec((1, T, C), tile),
            pl.BlockSpec((1, T, C), tile),
            pl.BlockSpec((RNN_CONV, C), chan),
            pl.BlockSpec((1, C), chan),
            pl.BlockSpec((1, C, 2 * C), lambda b, c, t: (c, 0, 0)),
            pl.BlockSpec((1, 1, 2 * C), lambda b, c, t: (c, 0, 0)),
            pl.BlockSpec((1, C), chan),
        ],
        out_specs=pl.BlockSpec((1, T, C), tile),
        out_shape=jax.ShapeDtypeStruct((b_sz, s_len, RNN_WIDTH), MXU_DTYPE),
        scratch_shapes=[
            pltpu.VMEM((SUBLANES, C), jnp.float32),
            pltpu.VMEM((SUBLANES, C), jnp.float32),
        ],
        compiler_params=_params(("arbitrary", "arbitrary", "arbitrary")),
        name="rglru",
    )(x3, y3, conv_w, conv_b, wg, bg, lam)
    return out.reshape(b_sz * s_len, RNN_WIDTH)


def _split_heads(qt):
    rows = lax.broadcasted_iota(jnp.int32, qt.shape, 0)
    zero = jnp.zeros((), qt.dtype)
    return [jnp.where(rows < HEAD_DIM, qt, zero), jnp.where(rows >= HEAD_DIM, qt, zero)]


def _moba_kernel(far_ref, qt_ref, k_ref, vt_ref, kmean_ref, bias_ref, o_ref, sel_ref, qs_ref, s0_ref, s1_ref,
                 s2_ref, s3_ref, pa_ref, pb_ref, ala_ref, alb_ref, m_ref, acc_ref):
    n = pl.program_id(2)
    MB = MOBA_BLOCK
    nb = kmean_ref.shape[1]
    qts = _split_heads(qt_ref[0, 0])

    kmean = kmean_ref[0].astype(MXU_DTYPE)
    blk = lax.broadcasted_iota(jnp.int32, (nb, MB), 0).astype(jnp.float32)
    n_f = n.astype(jnp.float32)
    past = blk < n_f
    for e in range(2):
        gate = _dot(kmean, qts[e])
        gate = jnp.where(past, gate, NEG_INF)
        sel = jnp.zeros((nb, MB), jnp.float32)
        for _ in range(MOBA_TOPK):
            mx = jnp.max(gate, axis=0, keepdims=True)
            first = jnp.min(jnp.where(gate == mx, blk, float(nb)), axis=0, keepdims=True)
            hit = blk == first
            sel = jnp.where(hit, 1.0, sel)
            gate = jnp.where(hit, TAKEN, gate)
        sel_ref[e] = jnp.where(blk == n_f, 1.0, jnp.where(past, sel, 0.0))

    for e in range(2):
        qs_ref[e] = qts[e]
    m_ref[...] = jnp.full(m_ref.shape, NEG_INF, jnp.float32)
    acc_ref[...] = jnp.zeros(acc_ref.shape, jnp.float32)
    pb_ref[...] = jnp.zeros(pb_ref.shape, pb_ref.dtype)
    alb_ref[...] = jnp.ones(alb_ref.shape, jnp.float32)
    ones_rows = jnp.ones((MOBA_SUM_ROWS, MB), MXU_DTYPE)

    def block_of(t):
        return jnp.clip(n - t, 0, n)

    def accumulate_issue(t, p_ref):
        vt_m = vt_ref[0, block_of(t)]
        pvs = []
        for e in range(2):
            lhs = jnp.concatenate([vt_m[e * HEAD_DIM:(e + 1) * HEAD_DIM, :], ones_rows], axis=0)
            pvs.append(_dot(lhs, p_ref[e]))
        return pvs

    def accumulate_finish(pvs, al_ref):
        for e in range(2):
            acc_ref[e] = al_ref[e] * acc_ref[e] + pvs[e]

    far_bias = [far_ref[2 * pl.program_id(1) + e] for e in range(2)]

    def scores(t, s_ref, far):
        m = block_of(t)
        k_m = k_ref[0, pl.ds(pl.multiple_of(m * MB, MB), MB), :]
        for e in range(2):
            s = _dot(k_m, qs_ref[e])
            s_ref[e] = s if far else s + bias_ref[e, t]

    def softmax(t, s_ref, p_ref, al_ref, far):
        m = block_of(t)
        live = t <= n
        for e in range(2):
            s = s_ref[e]
            chosen = (sel_ref[e, pl.ds(m, 1), :] > 0.0) & live
            mx = jnp.max(s, axis=0, keepdims=True)
            if far:
                mx = mx + far_bias[e]
            mx = jnp.where(chosen, mx, NEG_INF)
            m_old = m_ref[e]
            m_new = jnp.maximum(m_old, mx)
            al_ref[e] = jnp.exp(m_old - m_new)
            sub = m_new - far_bias[e] if far else m_new
            p_ref[e] = jnp.exp(s - jnp.where(chosen, sub, -NEG_INF)).astype(p_ref.dtype)
            m_ref[e] = m_new

    s_bufs = (s0_ref, s1_ref, s2_ref, s3_ref)
    p_bufs = ((pa_ref, ala_ref), (pb_ref, alb_ref))

    def region(t, slot, far, far_ahead):
        p_cur, al_cur = p_bufs[slot % 2]
        p_oth, al_oth = p_bufs[1 - slot % 2]
        pvs = accumulate_issue(t - 1, p_oth)
        scores(t + MOBA_AHEAD, s_bufs[(slot + MOBA_AHEAD) % MOBA_UNROLL], far_ahead)
        accumulate_finish(pvs, al_oth)
        softmax(t, s_bufs[slot], p_cur, al_cur, far)

    def near_regions(t_lo, t_hi):
        for t in range(t_lo, t_hi):
            region(t, t % MOBA_UNROLL, t >= MOBA_NEAR, t + MOBA_AHEAD >= MOBA_NEAR)

    near_steps = 2 * MOBA_UNROLL
    for t in range(MOBA_AHEAD):
        scores(t, s_bufs[t], False)
    near_regions(0, MOBA_UNROLL)

    @pl.when(n >= MOBA_UNROLL)
    def _():
        near_regions(MOBA_UNROLL, near_steps)

    def far_regions(t0, slot0, count):
        for r in range(count):
            region(t0 + r, slot0 + r, True, True)

    n_far = jnp.maximum(n + 1 - near_steps, 0)
    n_quads = (n_far + 1) // MOBA_UNROLL
    n_pairs = jnp.where(n_far > MOBA_UNROLL * n_quads, 1, 0)

    def quad_body(i, carry):
        far_regions(near_steps + MOBA_UNROLL * i, 0, MOBA_UNROLL)
        return carry

    lax.fori_loop(0, n_quads, quad_body, 0)
    t_pairs = near_steps + MOBA_UNROLL * n_quads

    @pl.when(n_pairs > 0)
    def _():
        far_regions(t_pairs, 0, 2)

    t_last = jnp.where(n < MOBA_UNROLL, MOBA_UNROLL - 1, t_pairs + 2 * n_pairs - 1)
    accumulate_finish(accumulate_issue(t_last, pb_ref), alb_ref)

    top = acc_ref[0, 0:HEAD_DIM, :] / acc_ref[0, HEAD_DIM:HEAD_DIM + 1, :]
    bot = acc_ref[1, 0:HEAD_DIM, :] / acc_ref[1, HEAD_DIM:HEAD_DIM + 1, :]
    o_ref[0] = jnp.concatenate([top, bot], axis=0).T.astype(o_ref.dtype)


def _moba_call(qct, kc, vct, kmean, bias_c, far_c, b_sz, s_len):
    MB = MOBA_BLOCK
    nb = s_len // MB
    k3 = kc.reshape(b_sz, s_len, MOBA_W)
    km3 = kmean.reshape(b_sz, nb, MOBA_W)
    npair = MOBA_W // LANES
    out = pl.pallas_call(
        _moba_kernel,
        grid=(b_sz, npair, nb),
        in_specs=[
            pl.BlockSpec(memory_space=pltpu.SMEM),
            pl.BlockSpec((1, 1, LANES, MB), lambda b, p, n: (b, n, p, 0)),
            pl.BlockSpec((1, s_len, LANES), lambda b, p, n: (b, 0, p)),
            pl.BlockSpec((1, nb, LANES, MB), lambda b, p, n: (b, 0, p, 0)),
            pl.BlockSpec((1, nb, LANES), lambda b, p, n: (b, 0, p)),
            pl.BlockSpec((2, MOBA_NEAR, MB, MB), lambda b, p, n: (p, 0, 0, 0)),
        ],
        out_specs=pl.BlockSpec((1, MB, LANES), lambda b, p, n: (b, n, p)),
        out_shape=jax.ShapeDtypeStruct((b_sz, s_len, MOBA_W), MXU_DTYPE),
        scratch_shapes=[
            pltpu.VMEM((2, nb, MB), jnp.float32),
            pltpu.VMEM((2, LANES, MB), MXU_DTYPE),
            pltpu.VMEM((2, MB, MB), jnp.float32),
            pltpu.VMEM((2, MB, MB), jnp.float32),
            pltpu.VMEM((2, MB, MB), jnp.float32),
            pltpu.VMEM((2, MB, MB), jnp.float32),
            pltpu.VMEM((2, MB, MB), MXU_DTYPE),
            pltpu.VMEM((2, MB, MB), MXU_DTYPE),
            pltpu.VMEM((2, 1, MB), jnp.float32),
            pltpu.VMEM((2, 1, MB), jnp.float32),
            pltpu.VMEM((2, 1, MB), jnp.float32),
            pltpu.VMEM((2, HEAD_DIM + MOBA_SUM_ROWS, MB), jnp.float32),
        ],
        compiler_params=_params(("arbitrary", "arbitrary", "arbitrary")),
        name="moba",
    )(far_c, qct, k3, vct, km3, bias_c)
    return out.reshape(b_sz * s_len, MOBA_W)


def _merge_kernel(x_ref, oa_ref, ob_ref, oc_ref, g_ref, mod_ref, wa_ref, wb_ref, wc_ref, wo_ref, o_ref):
    d = D_MODEL
    merged = _sigmoid(g_ref[:, 0:d]) * _dot(oa_ref[...], wa_ref[...])
    merged = merged + _sigmoid(g_ref[:, d:2 * d]) * _dot(ob_ref[...], wb_ref[...])
    merged = merged + _sigmoid(g_ref[:, 2 * d:3 * d]) * _dot(oc_ref[...], wc_ref[...])
    y = _dot(merged.astype(MXU_DTYPE), wo_ref[...])
    gate_m = mod_ref[0, :, 2 * d:3 * d]
    o_ref[...] = x_ref[...] + gate_m * y


def _merge_call(x2, o_a, o_b, o_c, g_logits, mods_l, wa, wb, wc, wo, s_len):
    n, d = x2.shape
    tm = min(TM_MERGE, s_len)
    tpb = s_len // tm
    row = lambda i: (i, 0)
    const2 = lambda i: (0, 0)
    return pl.pallas_call(
        _merge_kernel,
        grid=(n // tm,),
        in_specs=[
            pl.BlockSpec((tm, d), row),
            pl.BlockSpec((tm, SWA_Q), row),
            pl.BlockSpec((tm, RNN_WIDTH), row),
            pl.BlockSpec((tm, MOBA_W), row),
            pl.BlockSpec((tm, 3 * d), row),
            pl.BlockSpec((1, 1, 6 * d), lambda i: (i // tpb, 0, 0)),
            pl.BlockSpec(wa.shape, const2),
            pl.BlockSpec(wb.shape, const2),
            pl.BlockSpec(wc.shape, const2),
            pl.BlockSpec(wo.shape, const2),
        ],
        out_specs=pl.BlockSpec((tm, d), row),
        out_shape=jax.ShapeDtypeStruct((n, d), jnp.float32),
        compiler_params=_params(("arbitrary",)),
        name="merge",
    )(x2, o_a, o_b, o_c, g_logits, mods_l, wa, wb, wc, wo)


def _ffn_kernel(x_ref, mod_ref, gain_ref, wu_ref, cw_ref, cb_ref, wd_ref, o_ref, h_ref, act_ref, halo_ref,
                *, tiles_per_seq):
    i = pl.program_id(0)
    d = D_MODEL
    tm = x_ref.shape[0]
    H = SUBLANES
    CH = FFN_CHUNK

    @pl.when(i == 0)
    def _():
        halo_ref[...] = jnp.zeros(halo_ref.shape, jnp.float32)

    x = x_ref[...]
    shift = mod_ref[0, :, 3 * d:4 * d]
    scale = mod_ref[0, :, 4 * d:5 * d]
    ms = jnp.mean(x * x, axis=-1, keepdims=True)
    y = x * lax.rsqrt(ms + RMS_EPS) * gain_ref[...]
    h_ref[...] = (y * (1.0 + scale) + shift).astype(h_ref.dtype)

    seq_start = (i % tiles_per_seq) == 0
    row8 = lax.broadcasted_iota(jnp.int32, (H, CH), 0)

    def conv(col):
        cols = slice(col, col + CH)
        u = _dot(h_ref[...], wu_ref[:, cols])
        halo = jnp.where(seq_start, 0.0, halo_ref[:, cols])
        halo_ref[:, cols] = u[tm - H:tm, :]
        out = u * cw_ref[FFN_CONV - 1:FFN_CONV, cols] + cb_ref[:, cols]
        for k in range(1, FFN_CONV):
            out = out + _shift_rows(u, k, halo, row8) * cw_ref[FFN_CONV - 1 - k:FFN_CONV - k, cols]
        return out

    for c in range(D_FF // CH):
        ug = conv(c * CH)
        uv = conv(D_FF + c * CH)
        act_ref[:, c * CH:(c + 1) * CH] = (ug * _sigmoid(ug) * uv).astype(act_ref.dtype)

    gate_f = mod_ref[0, :, 5 * d:6 * d]
    o_ref[...] = x_ref[...] + gate_f * _dot(act_ref[...], wd_ref[...])


def _ffn_call(x2, mods_l, gain, w_up, conv_w, conv_b, w_down, s_len):
    n, d = x2.shape
    tm = min(TM_FFN, s_len)
    tpb = s_len // tm
    row = lambda i: (i, 0)
    const2 = lambda i: (0, 0)
    kern = functools.partial(_ffn_kernel, tiles_per_seq=tpb)
    return pl.pallas_call(
        kern,
        grid=(n // tm,),
        in_specs=[
            pl.BlockSpec((tm, d), row),
            pl.BlockSpec((1, 1, 6 * d), lambda i: (i // tpb, 0, 0)),
            pl.BlockSpec((1, d), const2),
            pl.BlockSpec(w_up.shape, const2, pipeline_mode=pl.Buffered(1)),
            pl.BlockSpec(conv_w.shape, const2),
            pl.BlockSpec(conv_b.shape, const2),
            pl.BlockSpec(w_down.shape, const2, pipeline_mode=pl.Buffered(1)),
        ],
        out_specs=pl.BlockSpec((tm, d), row),
        out_shape=jax.ShapeDtypeStruct((n, d), jnp.float32),
        scratch_shapes=[
            pltpu.VMEM((tm, d), MXU_DTYPE),
            pltpu.VMEM((tm, D_FF), MXU_DTYPE),
            pltpu.VMEM((SUBLANES, 2 * D_FF), jnp.float32),
        ],
        compiler_params=_params(("arbitrary",)),
        name="ffn",
    )(x2, mods_l, gain, w_up, conv_w, conv_b, w_down)


def _swa_head_order():
    order = []
    for c in range(SWA_Q // LANES):
        for g in range(SWA_KV_HEADS):
            order.append(g * SWA_GROUP + c)
    return order


def _bias_tables(rel_bias):
    L = SWA_BLOCK
    qi = np.arange(L)[:, None]
    sj = np.arange(2 * L)[None, :]
    bucket_a = _t5_bucket_np(qi + L - sj)[None]
    tab_a = _bias_tiles_call(rel_bias, bucket_a, 0, SWA_HEADS)[:, 0]
    MB = MOBA_BLOCK
    kj = np.arange(MB)[:, None]
    qq = np.arange(MB)[None, :]
    deltas = np.arange(MOBA_NEAR)[:, None, None]
    bucket_c = _t5_bucket_np(deltas * MB + qq[None] - kj[None])
    bucket_c = np.where((deltas > 0) | (kj <= qq)[None], bucket_c, NUM_BUCKETS)
    tab_c = _bias_tiles_call(rel_bias, bucket_c, SWA_HEADS, MOBA_HEADS)
    far_c = rel_bias[NUM_BUCKETS - 1, SWA_HEADS:]
    return tab_a, tab_c, far_c


def kernel(x, c, w_mod, b_mod, norm_mix, norm_ffn, w_in, qnorm_a, knorm_a, sinks, rnn_conv_w, rnn_conv_b,
           rnn_gate_a_w, rnn_gate_a_b, rnn_gate_x_w, rnn_gate_x_b, rnn_lambda, qnorm_c, knorm_c, rel_bias,
           w_branch, w_out, w_up, ffn_conv_w, ffn_conv_b, w_down):
    b_sz, s_len, d = x.shape
    depth = w_in.shape[0]
    assert d == D_MODEL and s_len % MOBA_BLOCK == 0 and b_sz <= SUBLANES
    n = b_sz * s_len
    bf = MXU_DTYPE
    f32 = jnp.float32

    c_pad = jnp.zeros((SUBLANES, d), f32).at[:b_sz].set(c)
    mods = _mod_call(c_pad, w_mod, b_mod)[:, :b_sz]
    mods = mods.reshape(depth, b_sz, 1, 6 * d)

    offs = np.cumsum([0, SWA_Q, SWA_KV, SWA_KV, RNN_WIDTH, RNN_WIDTH, MOBA_W, MOBA_W, MOBA_W, 3 * d])
    o_qa, o_ka, o_va, o_xr, o_yr, o_qc, o_kc, o_vc, o_g, o_end = [int(v) for v in offs]
    head_order = _swa_head_order()
    w_in_b = w_in.astype(bf)
    wn_all = jnp.concatenate(
        [w_in_b[:, :, o_qa + h * HEAD_DIM:o_qa + (h + 1) * HEAD_DIM] for h in head_order]
        + [w_in_b[:, :, o_ka:o_qc], w_in_b[:, :, o_kc:o_vc], w_in_b[:, :, o_g:o_end]], axis=-1)
    wt_all = jnp.swapaxes(jnp.concatenate([w_in_b[:, :, o_qc:o_kc], w_in_b[:, :, o_vc:o_g]], axis=-1), 1, 2)
    lane_head = np.arange(LANES) // HEAD_DIM
    ones_blk = jnp.asarray(lane_head[:, None] == lane_head[None, :], bf)

    wa_all = jnp.concatenate(
        [w_branch[:, h * HEAD_DIM:(h + 1) * HEAD_DIM] for h in head_order], axis=1).astype(bf)
    wb_all = w_branch[:, SWA_Q:SWA_Q + RNN_WIDTH].astype(bf)
    wc_all = w_branch[:, SWA_Q + RNN_WIDTH:].astype(bf)
    wo_all = w_out.astype(bf)
    wup_all = w_up.astype(bf)
    wdn_all = w_down.astype(bf)

    per = C_RNN // RNN_BLOCK_WIDTH
    nct = RNN_WIDTH // C_RNN
    eye = jnp.eye(per, dtype=f32)

    def blockdiag(w):
        w5 = w.reshape(depth, nct, per, RNN_BLOCK_WIDTH, RNN_BLOCK_WIDTH)
        full = jnp.einsum('lcpij,pq->lcpiqj', w5, eye)
        return full.reshape(depth, nct, C_RNN, C_RNN)

    wg_all = jnp.concatenate([blockdiag(rnn_gate_a_w), blockdiag(rnn_gate_x_w)], axis=-1).astype(bf)
    bg_all = jnp.concatenate([rnn_gate_a_b.reshape(depth, nct, 1, C_RNN),
                              rnn_gate_x_b.reshape(depth, nct, 1, C_RNN)], axis=-1)

    tab_a, tab_c, far_c = _bias_tables(rel_bias)

    x2 = x.reshape(n, d)
    for l in range(depth):
        gq_a = jnp.tile(qnorm_a[l], SWA_Q // HEAD_DIM)[None, :]
        gk_a = jnp.tile(knorm_a[l], SWA_KV // HEAD_DIM)[None, :]
        gk_c = jnp.tile(knorm_c[l], MOBA_W // HEAD_DIM)[None, :]
        gq_c = jnp.tile(qnorm_c[l], MOBA_W // HEAD_DIM)[:, None]
        qa, ka, va, xr, yr, kc, kmean, g_logits, qct, vct = _inproj_call(
            x2, mods[l], norm_mix[l][None, :], wn_all[l], wt_all[l], ones_blk, gq_a, gk_a, gk_c, gq_c,
            b_sz, s_len)
        o_a = _swa_call(qa, ka, va, sinks[l], tab_a, b_sz, s_len)
        o_b = _rnn_call(xr, yr, rnn_conv_w[l], rnn_conv_b[l][None, :], wg_all[l], bg_all[l],
                        rnn_lambda[l][None, :], b_sz, s_len)
        o_c = _moba_call(qct, kc, vct, kmean, tab_c, far_c, b_sz, s_len)
        x2 = _merge_call(x2, o_a, o_b, o_c, g_logits, mods[l], wa_all[l], wb_all[l], wc_all[l], wo_all[l],
                         s_len)
        x2 = _ffn_call(x2, mods[l], norm_ffn[l][None, :], wup_all[l], ffn_conv_w[l], ffn_conv_b[l][None, :],
                       wdn_all[l], s_len)
    return x2.reshape(b_sz, s_len, d)
```

```python
import functools
import math

import numpy as np
import jax
import jax.numpy as jnp
from jax import lax
from jax.experimental import pallas as pl
from jax.experimental.pallas import tpu as pltpu

D_MODEL = 1024
DEPTH = 4
HEAD_DIM = 64
SWA_HEADS = 8
SWA_KV_HEADS = 2
SWA_WINDOW = 128
SWA_BLOCK = 128
RNN_WIDTH = D_MODEL
RNN_BLOCKS = 16
RNN_BLOCK_WIDTH = RNN_WIDTH // RNN_BLOCKS
RNN_CONV = 4
RGLRU_C = 8.0
MOBA_HEADS = 8
MOBA_BLOCK = 256
MOBA_TOPK = 3
NUM_BUCKETS = 32
MAX_DISTANCE = 2048
D_FF = 2816
FFN_CONV = 3
RMS_EPS = 1e-6
NEG_INF = -1e30
TAKEN = -3e38
SWA_Q = SWA_HEADS * HEAD_DIM
SWA_KV = SWA_KV_HEADS * HEAD_DIM
MOBA_W = MOBA_HEADS * HEAD_DIM
SWA_GROUP = SWA_HEADS // SWA_KV_HEADS

LANES = 128
SUBLANES = 8
VMEM_LIMIT_BYTES = 56 * 1024 * 1024

MXU_DTYPE = jnp.bfloat16
SM_SCALE = HEAD_DIM ** -0.5

TM_IN = 512
TM_MERGE = 512
TM_FFN = 512
FFN_CHUNK = 256
T_RNN = 512
C_RNN = 256


def _params(semantics):
    return pltpu.CompilerParams(dimension_semantics=semantics, vmem_limit_bytes=VMEM_LIMIT_BYTES)


def _dot(a, b):
    return jnp.dot(a, b, preferred_element_type=jnp.float32)


def _dot_nt(a, b):
    return lax.dot_general(a, b, (((1,), (1,)), ((), ())), preferred_element_type=jnp.float32)


def _sigmoid(x):
    return 1.0 / (1.0 + jnp.exp(-x))


def _shift_rows(u, k, halo, row8):
    H = SUBLANES
    r = pltpu.roll(u, k, 0)
    top = jnp.where(row8 < k, pltpu.roll(halo, k, 0), r[0:H, :])
    return jnp.concatenate([top, r[H:, :]], axis=0)


def _t5_bucket_np(dist):
    dist = np.maximum(dist, 0)
    max_exact = NUM_BUCKETS // 2
    ratio = np.maximum(dist, 1).astype(np.float32) / np.float32(max_exact)
    log_ratio = np.log(ratio).astype(np.float32) / np.float32(math.log(MAX_DISTANCE / max_exact))
    large = max_exact + (log_ratio * np.float32(NUM_BUCKETS - max_exact)).astype(np.int32)
    large = np.minimum(large, NUM_BUCKETS - 1)
    return np.where(dist < max_exact, dist, large).astype(np.int32)


def _moba_near_blocks():
    delta = 0
    while True:
        lo = max(delta * MOBA_BLOCK - (MOBA_BLOCK - 1), 0)
        if int(_t5_bucket_np(np.array([lo]))[0]) == NUM_BUCKETS - 1:
            return delta
        delta += 1


MOBA_NEAR = _moba_near_blocks()
MOBA_SUM_ROWS = 2 * SUBLANES
MOBA_UNROLL = 4
MOBA_AHEAD = 1
assert MOBA_NEAR <= 2 * MOBA_UNROLL and MOBA_UNROLL == 4 and MOBA_AHEAD < MOBA_UNROLL


def _bias_tile_kernel(tab_ref, bucket_ref, o_ref, *, head0):
    h = pl.program_id(0) + head0
    bucket = bucket_ref[0]
    out = jnp.full(bucket.shape, NEG_INF, jnp.float32)
    for b in range(NUM_BUCKETS):
        out = jnp.where(bucket == b, tab_ref[b, h], out)
    o_ref[0, 0] = out


def _bias_tiles_call(rel_bias, bucket_np, head0, n_heads):
    n_tiles, r, c = bucket_np.shape
    return pl.pallas_call(
        functools.partial(_bias_tile_kernel, head0=head0),
        grid=(n_heads, n_tiles),
        in_specs=[
            pl.BlockSpec(memory_space=pltpu.SMEM),
            pl.BlockSpec((1, r, c), lambda h, t: (t, 0, 0)),
        ],
        out_specs=pl.BlockSpec((1, 1, r, c), lambda h, t: (h, t, 0, 0)),
        out_shape=jax.ShapeDtypeStruct((n_heads, n_tiles, r, c), jnp.float32),
        compiler_params=_params(("arbitrary", "arbitrary")),
        name="bias_tiles",
    )(rel_bias, jnp.asarray(bucket_np, jnp.int32))


def _mod_kernel(c_ref, w_ref, b_ref, o_ref):
    c = c_ref[...]
    c_act = (c * _sigmoid(c)).astype(MXU_DTYPE)
    o_ref[0] = _dot(c_act, w_ref[0].astype(MXU_DTYPE)) + b_ref[0]


def _mod_call(c_pad, w_mod, b_mod):
    depth, d, d6 = w_mod.shape
    tn = 1536
    return pl.pallas_call(
        _mod_kernel,
        grid=(depth, d6 // tn),
        in_specs=[
            pl.BlockSpec((SUBLANES, d), lambda l, j: (0, 0)),
            pl.BlockSpec((1, d, tn), lambda l, j: (l, 0, j)),
            pl.BlockSpec((1, 1, tn), lambda l, j: (l, 0, j)),
        ],
        out_specs=pl.BlockSpec((1, SUBLANES, tn), lambda l, j: (l, 0, j)),
        out_shape=jax.ShapeDtypeStruct((depth, SUBLANES, d6), jnp.float32),
        compiler_params=_params(("arbitrary", "arbitrary")),
        name="adaln_mod",
    )(c_pad, w_mod, b_mod.reshape(depth, 1, d6))


_C_QA = 0
_C_KVA = _C_QA + SWA_Q
_C_XR = _C_KVA + 2 * SWA_KV
_C_YR = _C_XR + RNN_WIDTH
_C_KC = _C_YR + RNN_WIDTH
_C_G = _C_KC + MOBA_W
_C_END = _C_G + 3 * D_MODEL


def _head_norm_rows(z, ones_blk, gain):
    zz = z * z
    hi = zz.astype(jnp.bfloat16)
    lo = (zz - hi.astype(jnp.float32)).astype(jnp.bfloat16)
    outs = []
    for c in range(z.shape[1] // LANES):
        sl = slice(c * LANES, (c + 1) * LANES)
        ss = _dot(hi[:, sl], ones_blk) + _dot(lo[:, sl], ones_blk)
        outs.append(z[:, sl] * lax.rsqrt(ss * (1.0 / HEAD_DIM) + RMS_EPS))
    y = outs[0] if len(outs) == 1 else jnp.concatenate(outs, axis=1)
    return y * gain


def _head_norm_cols(zt, gain_col):
    w, m = zt.shape
    z3 = zt.reshape(w // HEAD_DIM, HEAD_DIM, m)
    ss = jnp.sum(z3 * z3, axis=1, keepdims=True)
    y = z3 * lax.rsqrt(ss * (1.0 / HEAD_DIM) + RMS_EPS)
    return y.reshape(w, m) * gain_col


def _inproj_kernel(x_ref, mod_ref, gain_ref, wn_ref, wt_ref, ones_ref, gq_a_ref, gk_a_ref, gk_c_ref,
                   gq_c_ref, qa_ref, ka_ref, va_ref, xr_ref, yr_ref, kc_ref, kmean_ref, g_ref,
                   qct_ref, vct_ref):
    d = D_MODEL
    x = x_ref[...]
    shift = mod_ref[0, :, 0:d]
    scale = mod_ref[0, :, d:2 * d]
    ms = jnp.mean(x * x, axis=-1, keepdims=True)
    y = x * lax.rsqrt(ms + RMS_EPS) * gain_ref[...]
    h = (y * (1.0 + scale) + shift).astype(MXU_DTYPE)
    ones_blk = ones_ref[...]

    z = _dot(h, wn_ref[:, _C_QA:_C_KVA])
    qa_ref[...] = (_head_norm_rows(z, ones_blk, gq_a_ref[...]) * SM_SCALE).astype(qa_ref.dtype)

    z = _dot(h, wn_ref[:, _C_KVA:_C_XR])
    ka_ref[...] = _head_norm_rows(z[:, :SWA_KV], ones_blk, gk_a_ref[...]).astype(ka_ref.dtype)
    va_ref[...] = z[:, SWA_KV:].astype(va_ref.dtype)

    xr_ref[...] = _dot(h, wn_ref[:, _C_XR:_C_YR])
    yr_ref[...] = _dot(h, wn_ref[:, _C_YR:_C_KC])

    z = _dot(h, wn_ref[:, _C_KC:_C_G])
    kc = _head_norm_rows(z, ones_blk, gk_c_ref[...])
    kc_ref[...] = kc.astype(kc_ref.dtype)
    tm = x.shape[0]
    nblk = tm // MOBA_BLOCK
    kmean_ref[0] = jnp.mean(kc.reshape(nblk, MOBA_BLOCK, MOBA_W), axis=1)

    for c in range(3 * D_MODEL // 512):
        g_ref[:, c * 512:(c + 1) * 512] = _dot(h, wn_ref[:, _C_G + c * 512:_C_G + (c + 1) * 512])

    zt = _dot_nt(wt_ref[0:MOBA_W, :], h)
    qct = (_head_norm_cols(zt, gq_c_ref[...]) * SM_SCALE).astype(qct_ref.dtype)
    zt = _dot_nt(wt_ref[MOBA_W:2 * MOBA_W, :], h)
    vct = zt.astype(vct_ref.dtype)
    for r in range(nblk):
        qct_ref[0, r] = qct[:, r * MOBA_BLOCK:(r + 1) * MOBA_BLOCK]
        vct_ref[0, r] = vct[:, r * MOBA_BLOCK:(r + 1) * MOBA_BLOCK]


def _inproj_call(x2, mods_l, gain, wn, wt, ones_blk, gq_a, gk_a, gk_c, gq_c, b_sz, s_len):
    n, d = x2.shape
    tm = min(TM_IN, s_len)
    tpb = s_len // tm
    nblk = tm // MOBA_BLOCK
    nb = s_len // MOBA_BLOCK
    row = lambda i: (i, 0)
    const2 = lambda i: (0, 0)
    bf = MXU_DTYPE
    out_shape = (
        jax.ShapeDtypeStruct((n, SWA_Q), bf),
        jax.ShapeDtypeStruct((n, SWA_KV), bf),
        jax.ShapeDtypeStruct((n, SWA_KV), bf),
        jax.ShapeDtypeStruct((n, RNN_WIDTH), jnp.float32),
        jax.ShapeDtypeStruct((n, RNN_WIDTH), jnp.float32),
        jax.ShapeDtypeStruct((n, MOBA_W), bf),
        jax.ShapeDtypeStruct((n // tm, nblk, MOBA_W), jnp.float32),
        jax.ShapeDtypeStruct((n, 3 * D_MODEL), jnp.float32),
        jax.ShapeDtypeStruct((b_sz, nb, MOBA_W, MOBA_BLOCK), bf),
        jax.ShapeDtypeStruct((b_sz, nb, MOBA_W, MOBA_BLOCK), bf),
    )
    blk4 = pl.BlockSpec((1, nblk, MOBA_W, MOBA_BLOCK), lambda i: (i // tpb, i % tpb, 0, 0))
    out_specs = (
        pl.BlockSpec((tm, SWA_Q), row),
        pl.BlockSpec((tm, SWA_KV), row),
        pl.BlockSpec((tm, SWA_KV), row),
        pl.BlockSpec((tm, RNN_WIDTH), row),
        pl.BlockSpec((tm, RNN_WIDTH), row),
        pl.BlockSpec((tm, MOBA_W), row),
        pl.BlockSpec((1, nblk, MOBA_W), lambda i: (i, 0, 0)),
        pl.BlockSpec((tm, 3 * D_MODEL), row),
        blk4,
        blk4,
    )
    in_specs = [
        pl.BlockSpec((tm, d), row),
        pl.BlockSpec((1, 1, 6 * d), lambda i: (i // tpb, 0, 0)),
        pl.BlockSpec((1, d), const2),
        pl.BlockSpec(wn.shape, const2, pipeline_mode=pl.Buffered(1)),
        pl.BlockSpec(wt.shape, const2, pipeline_mode=pl.Buffered(1)),
        pl.BlockSpec(ones_blk.shape, const2),
        pl.BlockSpec(gq_a.shape, const2),
        pl.BlockSpec(gk_a.shape, const2),
        pl.BlockSpec(gk_c.shape, const2),
        pl.BlockSpec(gq_c.shape, const2),
    ]
    return pl.pallas_call(
        _inproj_kernel,
        grid=(n // tm,),
        in_specs=in_specs,
        out_specs=out_specs,
        out_shape=out_shape,
        compiler_params=_params(("arbitrary",)),
        name="inproj",
    )(x2, mods_l, gain, wn, wt, ones_blk, gq_a, gk_a, gk_c, gq_c)


def _swa_kernel(sink_ref, q_ref, kp_ref, ko_ref, vp_ref, vo_ref, bias_ref, o_ref, s_ref, p_ref):
    n = pl.program_id(1)
    L = SWA_BLOCK
    kcat = jnp.concatenate([kp_ref[0], ko_ref[0]], axis=0)
    vcat = jnp.concatenate([vp_ref[0], vo_ref[0]], axis=0)
    qi = lax.broadcasted_iota(jnp.int32, (L, 2 * L), 0)
    sj = lax.broadcasted_iota(jnp.int32, (L, 2 * L), 1)
    diff = qi + L - sj
    valid = (diff >= 0) & (diff < SWA_WINDOW) & ((n > 0) | (sj >= L))
    lane = lax.broadcasted_iota(jnp.int32, (L, LANES), 1)
    low = lane < HEAD_DIM
    zero = jnp.zeros((), q_ref.dtype)
    for c in range(SWA_Q // LANES):
        q2 = q_ref[0, :, c * LANES:(c + 1) * LANES]
        for g in range(SWA_KV_HEADS):
            head = g * SWA_GROUP + c
            qm = jnp.where(low if g == 0 else ~low, q2, zero)
            s_ref[head] = jnp.where(valid, _dot_nt(qm, kcat) + bias_ref[head], NEG_INF)
    for head in range(SWA_HEADS):
        s = s_ref[head]
        sink = sink_ref[head]
        m = jnp.maximum(jnp.max(s, axis=-1, keepdims=True), sink)
        p = jnp.exp(s - m)
        denom = jnp.sum(p, axis=-1, keepdims=True) + jnp.exp(sink - m)
        p_ref[head] = (p / denom).astype(p_ref.dtype)
    for c in range(SWA_Q // LANES):
        halves = [_dot(p_ref[g * SWA_GROUP + c], vcat) for g in range(SWA_KV_HEADS)]
        o_ref[0, :, c * LANES:(c + 1) * LANES] = jnp.where(low, halves[0], halves[1]).astype(o_ref.dtype)


def _swa_call(qa, ka, va, sinks_l, bias_a, b_sz, s_len):
    L = SWA_BLOCK
    nblk = s_len // L
    q3 = qa.reshape(b_sz, s_len, SWA_Q)
    k3 = ka.reshape(b_sz, s_len, SWA_KV)
    v3 = va.reshape(b_sz, s_len, SWA_KV)
    prev = lambda b, n: (b, jnp.maximum(n - 1, 0), 0)
    own = lambda b, n: (b, n, 0)
    out = pl.pallas_call(
        _swa_kernel,
        grid=(b_sz, nblk),
        in_specs=[
            pl.BlockSpec(memory_space=pltpu.SMEM),
            pl.BlockSpec((1, L, SWA_Q), own),
            pl.BlockSpec((1, L, SWA_KV), prev),
            pl.BlockSpec((1, L, SWA_KV), own),
            pl.BlockSpec((1, L, SWA_KV), prev),
            pl.BlockSpec((1, L, SWA_KV), own),
            pl.BlockSpec((SWA_HEADS, L, 2 * L), lambda b, n: (0, 0, 0)),
        ],
        out_specs=pl.BlockSpec((1, L, SWA_Q), own),
        out_shape=jax.ShapeDtypeStruct((b_sz, s_len, SWA_Q), MXU_DTYPE),
        scratch_shapes=[
            pltpu.VMEM((SWA_HEADS, L, 2 * L), jnp.float32),
            pltpu.VMEM((SWA_HEADS, L, 2 * L), MXU_DTYPE),
        ],
        compiler_params=_params(("arbitrary", "arbitrary")),
        name="swa",
    )(sinks_l, q3, k3, k3, v3, v3, bias_a)
    return out.reshape(b_sz * s_len, SWA_Q)


def _gelu_tanh(x):
    return 0.5 * x * (1.0 + jnp.tanh(math.sqrt(2.0 / math.pi) * (x + 0.044715 * (x * x * x))))


def _rnn_kernel(xr_ref, yr_ref, cw_ref, cb_ref, wg_ref, bg_ref, lam_ref, o_ref, xbuf_ref, h_ref):
    t = pl.program_id(2)
    T = xr_ref.shape[1]
    C = xr_ref.shape[2]
    H = SUBLANES

    @pl.when(t == 0)
    def _():
        xbuf_ref[...] = jnp.zeros(xbuf_ref.shape, jnp.float32)
        h_ref[...] = jnp.zeros(h_ref.shape, jnp.float32)

    x = xr_ref[0]
    halo = xbuf_ref[...]
    row8 = lax.broadcasted_iota(jnp.int32, (H, C), 0)
    xc = x * cw_ref[RNN_CONV - 1:RNN_CONV, :] + cb_ref[...]
    for j in range(1, RNN_CONV):
        xc = xc + _shift_rows(x, j, halo, row8) * cw_ref[RNN_CONV - 1 - j:RNN_CONV - j, :]
    xbuf_ref[...] = x[T - H:T, :]

    gates = _dot(xc.astype(MXU_DTYPE), wg_ref[0]) + bg_ref[0]
    r = _sigmoid(gates[:, :C])
    i = _sigmoid(gates[:, C:])
    nlam = -lam_ref[...]
    softplus = jnp.maximum(nlam, 0.0) + jnp.log1p(jnp.exp(-jnp.abs(nlam)))
    log_a = (-RGLRU_C) * r * softplus
    a = jnp.exp(log_a)
    mult = jnp.sqrt(jnp.tanh(-log_a) * (1.0 + a * a))
    row = lax.broadcasted_iota(jnp.int32, (T, C), 0)
    mult = jnp.where((row == 0) & (t == 0), 1.0, mult)
    u = mult * (i * xc)

    G = T // H
    a = a.reshape(G, H, C)
    u = u.reshape(G, H, C)
    sub = lax.broadcasted_iota(jnp.int32, (G, H, C), 1)
    step = 1
    while step < H:
        keep = sub >= step
        a_prev = jnp.where(keep, pltpu.roll(a, step, 1), 1.0)
        u_prev = jnp.where(keep, pltpu.roll(u, step, 1), 0.0)
        u = a * u_prev + u
        a = a * a_prev
        step *= 2
    carry = h_ref[0:1, :]
    groups = []
    for g in range(G):
        hg = u[g] + a[g] * carry
        groups.append(hg)
        carry = hg[H - 1:H, :]
    h_ref[0:1, :] = carry
    hs = jnp.concatenate(groups, axis=0)
    o_ref[0] = (hs * _gelu_tanh(yr_ref[0])).astype(o_ref.dtype)


def _rnn_call(xr, yr, conv_w, conv_b, wg, bg, lam, b_sz, s_len):
    T = min(T_RNN, s_len)
    C = C_RNN
    nc = RNN_WIDTH // C
    x3 = xr.reshape(b_sz, s_len, RNN_WIDTH)
    y3 = yr.reshape(b_sz, s_len, RNN_WIDTH)
    tile = lambda b, c, t: (b, t, c)
    chan = lambda b, c, t: (0, c)
    out = pl.pallas_call(
        _rnn_kernel,
        grid=(b_sz, nc, s_len // T),
        in_specs=[
            pl.BlockSpec((1, T, C), tile),
            pl.BlockSpec((1, T, C), tile),
            pl.BlockSpec((RNN_CONV, C), chan),
            pl.BlockSpec((1, C), chan),
            pl.BlockSpec((1, C, 2 * C), lambda b, c, t: (c, 0, 0)),
            pl.BlockSpec((1, 1, 2 * C), lambda b, c, t: (c, 0, 0)),
            pl.BlockSpec((1, C), chan),
        ],
        out_specs=pl.BlockSpec((1, T, C), tile),
        out_shape=jax.ShapeDtypeStruct((b_sz, s_len, RNN_WIDTH), MXU_DTYPE),
        scratch_shapes=[
            pltpu.VMEM((SUBLANES, C), jnp.float32),
            pltpu.VMEM((SUBLANES, C), jnp.float32),
        ],
        compiler_params=_params(("arbitrary", "arbitrary", "arbitrary")),
        name="rglru",
    )(x3, y3, conv_w, conv_b, wg, bg, lam)
    return out.reshape(b_sz * s_len, RNN_WIDTH)


def _split_heads(qt):
    rows = lax.broadcasted_iota(jnp.int32, qt.shape, 0)
    zero = jnp.zeros((), qt.dtype)
    return [jnp.where(rows < HEAD_DIM, qt, zero), jnp.where(rows >= HEAD_DIM, qt, zero)]


def _moba_kernel(far_ref, qt_ref, k_ref, vt_ref, kmean_ref, bias_ref, o_ref, sel_ref, qs_ref, s0_ref, s1_ref,
                 s2_ref, s3_ref, mx_ref, pva_ref, pvb_ref, ala_ref, alb_ref, m_ref, acc_ref):
    n = pl.program_id(2)
    MB = MOBA_BLOCK
    nb = kmean_ref.shape[1]
    qts = _split_heads(qt_ref[0, 0])

    kmean = kmean_ref[0].astype(MXU_DTYPE)
    blk = lax.broadcasted_iota(jnp.int32, (nb, MB), 0).astype(jnp.float32)
    n_f = n.astype(jnp.float32)
    past = blk < n_f
    for e in range(2):
        gate = _dot(kmean, qts[e])
        gate = jnp.where(past, gate, NEG_INF)
        sel = jnp.zeros((nb, MB), jnp.float32)
        for _ in range(MOBA_TOPK):
            mx = jnp.max(gate, axis=0, keepdims=True)
            first = jnp.min(jnp.where(gate == mx, blk, float(nb)), axis=0, keepdims=True)
            hit = blk == first
            sel = jnp.where(hit, 1.0, sel)
            gate = jnp.where(hit, TAKEN, gate)
        sel_ref[e] = jnp.where(blk == n_f, 1.0, jnp.where(past, sel, 0.0))

    for e in range(2):
        qs_ref[e] = qts[e]
    m_ref[...] = jnp.full(m_ref.shape, NEG_INF, jnp.float32)
    acc_ref[...] = jnp.zeros(acc_ref.shape, jnp.float32)
    pvb_ref[...] = jnp.zeros(pvb_ref.shape, jnp.float32)
    alb_ref[...] = jnp.ones(alb_ref.shape, jnp.float32)
    ones_rows = jnp.ones((MOBA_SUM_ROWS, MB), MXU_DTYPE)

    def block_of(t):
        return jnp.clip(n - t, 0, n)

    def accumulate(pv_ref, al_ref):
        for e in range(2):
            acc_ref[e] = al_ref[e] * acc_ref[e] + pv_ref[e]

    far_bias = [far_ref[2 * pl.program_id(1) + e] for e in range(2)]

    s_bufs = (s0_ref, s1_ref, s2_ref, s3_ref)

    def scores(t, slot, far):
        m = block_of(t)
        k_m = k_ref[0, pl.ds(pl.multiple_of(m * MB, MB), MB), :]
        for e in range(2):
            s = _dot(k_m, qs_ref[e])
            if not far:
                s = s + bias_ref[e, t]
            s_bufs[slot][e] = s
            part = s[0:SUBLANES, :]
            for g in range(1, MB // SUBLANES):
                part = jnp.maximum(part, s[g * SUBLANES:(g + 1) * SUBLANES, :])
            mx_ref[slot, e] = part

    def softmax(t, slot, pv_ref, al_ref, far):
        m = block_of(t)
        live = t <= n
        vt_m = vt_ref[0, m]
        for e in range(2):
            s = s_bufs[slot][e]
            chosen = (sel_ref[e, pl.ds(m, 1), :] > 0.0) & live
            mx = jnp.max(mx_ref[slot, e], axis=0, keepdims=True)
            if far:
                mx = mx + far_bias[e]
            mx = jnp.where(chosen, mx, NEG_INF)
            m_old = m_ref[e]
            m_new = jnp.maximum(m_old, mx)
            al_ref[e] = jnp.exp(m_old - m_new)
            sub = m_new - far_bias[e] if far else m_new
            p = jnp.exp(s - jnp.where(chosen, sub, -NEG_INF)).astype(MXU_DTYPE)
            m_ref[e] = m_new
            lhs = jnp.concatenate([vt_m[e * HEAD_DIM:(e + 1) * HEAD_DIM, :], ones_rows], axis=0)
            pv_ref[e] = _dot(lhs, p)

    pv_bufs = ((pva_ref, ala_ref), (pvb_ref, alb_ref))

    def region(t, slot, far, far_ahead):
        pv_cur, al_cur = pv_bufs[slot % 2]
        pv_oth, al_oth = pv_bufs[1 - slot % 2]
        scores(t + MOBA_AHEAD, (slot + MOBA_AHEAD) % MOBA_UNROLL, far_ahead)
        accumulate(pv_oth, al_oth)
        softmax(t, slot, pv_cur, al_cur, far)

    def near_regions(t_lo, t_hi):
        for t in range(t_lo, t_hi):
            region(t, t % MOBA_UNROLL, t >= MOBA_NEAR, t + MOBA_AHEAD >= MOBA_NEAR)

    near_steps = 2 * MOBA_UNROLL
    for t in range(MOBA_AHEAD):
        scores(t, t, False)
    near_regions(0, MOBA_UNROLL)

    @pl.when(n >= MOBA_UNROLL)
    def _():
        near_regions(MOBA_UNROLL, near_steps)

    def far_regions(t0, slot0, count):
        for r in range(count):
            region(t0 + r, slot0 + r, True, True)

    n_far = jnp.maximum(n + 1 - near_steps, 0)
    n_quads = (n_far + 1) // MOBA_UNROLL
    n_pairs = jnp.where(n_far > MOBA_UNROLL * n_quads, 1, 0)

    def quad_body(i, carry):
        far_regions(near_steps + MOBA_UNROLL * i, 0, MOBA_UNROLL)
        return carry

    lax.fori_loop(0, n_quads, quad_body, 0)
    t_pairs = near_steps + MOBA_UNROLL * n_quads

    @pl.when(n_pairs > 0)
    def _():
        far_regions(t_pairs, 0, 2)

    accumulate(pvb_ref, alb_ref)

    top = acc_ref[0, 0:HEAD_DIM, :] / acc_ref[0, HEAD_DIM:HEAD_DIM + 1, :]
    bot = acc_ref[1, 0:HEAD_DIM, :] / acc_ref[1, HEAD_DIM:HEAD_DIM + 1, :]
    o_ref[0] = jnp.concatenate([top, bot], axis=0).T.astype(o_ref.dtype)


def _moba_call(qct, kc, vct, kmean, bias_c, far_c, b_sz, s_len):
    MB = MOBA_BLOCK
    nb = s_len // MB
    k3 = kc.reshape(b_sz, s_len, MOBA_W)
    km3 = kmean.reshape(b_sz, nb, MOBA_W)
    npair = MOBA_W // LANES
    out = pl.pallas_call(
        _moba_kernel,
        grid=(b_sz, npair, nb),
        in_specs=[
            pl.BlockSpec(memory_space=pltpu.SMEM),
            pl.BlockSpec((1, 1, LANES, MB), lambda b, p, n: (b, n, p, 0)),
            pl.BlockSpec((1, s_len, LANES), lambda b, p, n: (b, 0, p)),
            pl.BlockSpec((1, nb, LANES, MB), lambda b, p, n: (b, 0, p, 0)),
            pl.BlockSpec((1, nb, LANES), lambda b, p, n: (b, 0, p)),
            pl.BlockSpec((2, MOBA_NEAR, MB, MB), lambda b, p, n: (p, 0, 0, 0)),
        ],
        out_specs=pl.BlockSpec((1, MB, LANES), lambda b, p, n: (b, n, p)),
        out_shape=jax.ShapeDtypeStruct((b_sz, s_len, MOBA_W), MXU_DTYPE),
        scratch_shapes=[
            pltpu.VMEM((2, nb, MB), jnp.float32),
            pltpu.VMEM((2, LANES, MB), MXU_DTYPE),
            pltpu.VMEM((2, MB, MB), jnp.float32),
            pltpu.VMEM((2, MB, MB), jnp.float32),
            pltpu.VMEM((2, MB, MB), jnp.float32),
            pltpu.VMEM((2, MB, MB), jnp.float32),
            pltpu.VMEM((MOBA_UNROLL, 2, SUBLANES, MB), jnp.float32),
            pltpu.VMEM((2, HEAD_DIM + MOBA_SUM_ROWS, MB), jnp.float32),
            pltpu.VMEM((2, HEAD_DIM + MOBA_SUM_ROWS, MB), jnp.float32),
            pltpu.VMEM((2, 1, MB), jnp.float32),
            pltpu.VMEM((2, 1, MB), jnp.float32),
            pltpu.VMEM((2, 1, MB), jnp.float32),
            pltpu.VMEM((2, HEAD_DIM + MOBA_SUM_ROWS, MB), jnp.float32),
        ],
        compiler_params=_params(("arbitrary", "arbitrary", "arbitrary")),
        name="moba",
    )(far_c, qct, k3, vct, km3, bias_c)
    return out.reshape(b_sz * s_len, MOBA_W)


def _merge_kernel(x_ref, oa_ref, ob_ref, oc_ref, g_ref, mod_ref, wa_ref, wb_ref, wc_ref, wo_ref, o_ref):
    d = D_MODEL
    merged = _sigmoid(g_ref[:, 0:d]) * _dot(oa_ref[...], wa_ref[...])
    merged = merged + _sigmoid(g_ref[:, d:2 * d]) * _dot(ob_ref[...], wb_ref[...])
    merged = merged + _sigmoid(g_ref[:, 2 * d:3 * d]) * _dot(oc_ref[...], wc_ref[...])
    y = _dot(merged.astype(MXU_DTYPE), wo_ref[...])
    gate_m = mod_ref[0, :, 2 * d:3 * d]
    o_ref[...] = x_ref[...] + gate_m * y


def _merge_call(x2, o_a, o_b, o_c, g_logits, mods_l, wa, wb, wc, wo, s_len):
    n, d = x2.shape
    tm = min(TM_MERGE, s_len)
    tpb = s_len // tm
    row = lambda i: (i, 0)
    const2 = lambda i: (0, 0)
    return pl.pallas_call(
        _merge_kernel,
        grid=(n // tm,),
        in_specs=[
            pl.BlockSpec((tm, d), row),
            pl.BlockSpec((tm, SWA_Q), row),
            pl.BlockSpec((tm, RNN_WIDTH), row),
            pl.BlockSpec((tm, MOBA_W), row),
            pl.BlockSpec((tm, 3 * d), row),
            pl.BlockSpec((1, 1, 6 * d), lambda i: (i // tpb, 0, 0)),
            pl.BlockSpec(wa.shape, const2),
            pl.BlockSpec(wb.shape, const2),
            pl.BlockSpec(wc.shape, const2),
            pl.BlockSpec(wo.shape, const2),
        ],
        out_specs=pl.BlockSpec((tm, d), row),
        out_shape=jax.ShapeDtypeStruct((n, d), jnp.float32),
        compiler_params=_params(("arbitrary",)),
        name="merge",
    )(x2, o_a, o_b, o_c, g_logits, mods_l, wa, wb, wc, wo)


def _ffn_kernel(x_ref, mod_ref, gain_ref, wu_ref, cw_ref, cb_ref, wd_ref, o_ref, h_ref, act_ref, halo_ref,
                *, tiles_per_seq):
    i = pl.program_id(0)
    d = D_MODEL
    tm = x_ref.shape[0]
    H = SUBLANES
    CH = FFN_CHUNK

    @pl.when(i == 0)
    def _():
        halo_ref[...] = jnp.zeros(halo_ref.shape, jnp.float32)

    x = x_ref[...]
    shift = mod_ref[0, :, 3 * d:4 * d]
    scale = mod_ref[0, :, 4 * d:5 * d]
    ms = jnp.mean(x * x, axis=-1, keepdims=True)
    y = x * lax.rsqrt(ms + RMS_EPS) * gain_ref[...]
    h_ref[...] = (y * (1.0 + scale) + shift).astype(h_ref.dtype)

    seq_start = (i % tiles_per_seq) == 0
    row8 = lax.broadcasted_iota(jnp.int32, (H, CH), 0)

    def conv(col):
        cols = slice(col, col + CH)
        u = _dot(h_ref[...], wu_ref[:, cols])
        halo = jnp.where(seq_start, 0.0, halo_ref[:, cols])
        halo_ref[:, cols] = u[tm - H:tm, :]
        out = u * cw_ref[FFN_CONV - 1:FFN_CONV, cols] + cb_ref[:, cols]
        for k in range(1, FFN_CONV):
            out = out + _shift_rows(u, k, halo, row8) * cw_ref[FFN_CONV - 1 - k:FFN_CONV - k, cols]
        return out

    for c in range(D_FF // CH):
        ug = conv(c * CH)
        uv = conv(D_FF + c * CH)
        act_ref[:, c * CH:(c + 1) * CH] = (ug * _sigmoid(ug) * uv).astype(act_ref.dtype)

    gate_f = mod_ref[0, :, 5 * d:6 * d]
    o_ref[...] = x_ref[...] + gate_f * _dot(act_ref[...], wd_ref[...])


def _ffn_call(x2, mods_l, gain, w_up, conv_w, conv_b, w_down, s_len):
    n, d = x2.shape
    tm = min(TM_FFN, s_len)
    tpb = s_len // tm
    row = lambda i: (i, 0)
    const2 = lambda i: (0, 0)
    kern = functools.partial(_ffn_kernel, tiles_per_seq=tpb)
    return pl.pallas_call(
        kern,
        grid=(n // tm,),
        in_specs=[
            pl.BlockSpec((tm, d), row),
            pl.BlockSpec((1, 1, 6 * d), lambda i: (i // tpb, 0, 0)),
            pl.BlockSpec((1, d), const2),
            pl.BlockSpec(w_up.shape, const2, pipeline_mode=pl.Buffered(1)),
            pl.BlockSpec(conv_w.shape, const2),
            pl.BlockSpec(conv_b.shape, const2),
            pl.BlockSpec(w_down.shape, const2, pipeline_mode=pl.Buffered(1)),
        ],
        out_specs=pl.BlockSpec((tm, d), row),
        out_shape=jax.ShapeDtypeStruct((n, d), jnp.float32),
        scratch_shapes=[
            pltpu.VMEM((tm, d), MXU_DTYPE),
            pltpu.VMEM((tm, D_FF), MXU_DTYPE),
            pltpu.VMEM((SUBLANES, 2 * D_FF), jnp.float32),
        ],
        compiler_params=_params(("arbitrary",)),
        name="ffn",
    )(x2, mods_l, gain, w_up, conv_w, conv_b, w_down)


def _swa_head_order():
    order = []
    for c in range(SWA_Q // LANES):
        for g in range(SWA_KV_HEADS):
            order.append(g * SWA_GROUP + c)
    return order


def _bias_tables(rel_bias):
    L = SWA_BLOCK
    qi = np.arange(L)[:, None]
    sj = np.arange(2 * L)[None, :]
    bucket_a = _t5_bucket_np(qi + L - sj)[None]
    tab_a = _bias_tiles_call(rel_bias, bucket_a, 0, SWA_HEADS)[:, 0]
    MB = MOBA_BLOCK
    kj = np.arange(MB)[:, None]
    qq = np.arange(MB)[None, :]
    deltas = np.arange(MOBA_NEAR)[:, None, None]
    bucket_c = _t5_bucket_np(deltas * MB + qq[None] - kj[None])
    bucket_c = np.where((deltas > 0) | (kj <= qq)[None], bucket_c, NUM_BUCKETS)
    tab_c = _bias_tiles_call(rel_bias, bucket_c, SWA_HEADS, MOBA_HEADS)
    far_c = rel_bias[NUM_BUCKETS - 1, SWA_HEADS:]
    return tab_a, tab_c, far_c


def kernel(x, c, w_mod, b_mod, norm_mix, norm_ffn, w_in, qnorm_a, knorm_a, sinks, rnn_conv_w, rnn_conv_b,
           rnn_gate_a_w, rnn_gate_a_b, rnn_gate_x_w, rnn_gate_x_b, rnn_lambda, qnorm_c, knorm_c, rel_bias,
           w_branch, w_out, w_up, ffn_conv_w, ffn_conv_b, w_down):
    b_sz, s_len, d = x.shape
    depth = w_in.shape[0]
    assert d == D_MODEL and s_len % MOBA_BLOCK == 0 and b_sz <= SUBLANES
    n = b_sz * s_len
    bf = MXU_DTYPE
    f32 = jnp.float32

    c_pad = jnp.zeros((SUBLANES, d), f32).at[:b_sz].set(c)
    mods = _mod_call(c_pad, w_mod, b_mod)[:, :b_sz]
    mods = mods.reshape(depth, b_sz, 1, 6 * d)

    offs = np.cumsum([0, SWA_Q, SWA_KV, SWA_KV, RNN_WIDTH, RNN_WIDTH, MOBA_W, MOBA_W, MOBA_W, 3 * d])
    o_qa, o_ka, o_va, o_xr, o_yr, o_qc, o_kc, o_vc, o_g, o_end = [int(v) for v in offs]
    head_order = _swa_head_order()
    w_in_b = w_in.astype(bf)
    wn_all = jnp.concatenate(
        [w_in_b[:, :, o_qa + h * HEAD_DIM:o_qa + (h + 1) * HEAD_DIM] for h in head_order]
        + [w_in_b[:, :, o_ka:o_qc], w_in_b[:, :, o_kc:o_vc], w_in_b[:, :, o_g:o_end]], axis=-1)
    wt_all = jnp.swapaxes(jnp.concatenate([w_in_b[:, :, o_qc:o_kc], w_in_b[:, :, o_vc:o_g]], axis=-1), 1, 2)
    lane_head = np.arange(LANES) // HEAD_DIM
    ones_blk = jnp.asarray(lane_head[:, None] == lane_head[None, :], bf)

    wa_all = jnp.concatenate(
        [w_branch[:, h * HEAD_DIM:(h + 1) * HEAD_DIM] for h in head_order], axis=1).astype(bf)
    wb_all = w_branch[:, SWA_Q:SWA_Q + RNN_WIDTH].astype(bf)
    wc_all = w_branch[:, SWA_Q + RNN_WIDTH:].astype(bf)
    wo_all = w_out.astype(bf)
    wup_all = w_up.astype(bf)
    wdn_all = w_down.astype(bf)

    per = C_RNN // RNN_BLOCK_WIDTH
    nct = RNN_WIDTH // C_RNN
    eye = jnp.eye(per, dtype=f32)

    def blockdiag(w):
        w5 = w.reshape(depth, nct, per, RNN_BLOCK_WIDTH, RNN_BLOCK_WIDTH)
        full = jnp.einsum('lcpij,pq->lcpiqj', w5, eye)
        return full.reshape(depth, nct, C_RNN, C_RNN)

    wg_all = jnp.concatenate([blockdiag(rnn_gate_a_w), blockdiag(rnn_gate_x_w)], axis=-1).astype(bf)
    bg_all = jnp.concatenate([rnn_gate_a_b.reshape(depth, nct, 1, C_RNN),
                              rnn_gate_x_b.reshape(depth, nct, 1, C_RNN)], axis=-1)

    tab_a, tab_c, far_c = _bias_tables(rel_bias)

    x2 = x.reshape(n, d)
    for l in range(depth):
        gq_a = jnp.tile(qnorm_a[l], SWA_Q // HEAD_DIM)[None, :]
        gk_a = jnp.tile(knorm_a[l], SWA_KV // HEAD_DIM)[None, :]
        gk_c = jnp.tile(knorm_c[l], MOBA_W // HEAD_DIM)[None, :]
        gq_c = jnp.tile(qnorm_c[l], MOBA_W // HEAD_DIM)[:, None]
        qa, ka, va, xr, yr, kc, kmean, g_logits, qct, vct = _inproj_call(
            x2, mods[l], norm_mix[l][None, :], wn_all[l], wt_all[l], ones_blk, gq_a, gk_a, gk_c, gq_c,
            b_sz, s_len)
        o_a = _swa_call(qa, ka, va, sinks[l], tab_a, b_sz, s_len)
        o_b = _rnn_call(xr, yr, rnn_conv_w[l], rnn_conv_b[l][None, :], wg_all[l], bg_all[l],
                        rnn_lambda[l][None, :], b_sz, s_len)
        o_c = _moba_call(qct, kc, vct, kmean, tab_c, far_c, b_sz, s_len)
        x2 = _merge_call(x2, o_a, o_b, o_c, g_logits, mods[l], wa_all[l], wb_all[l], wc_all[l], wo_all[l],
                         s_len)
        x2 = _ffn_call(x2, mods[l], norm_ffn[l][None, :], wup_all[l], ffn_conv_w[l], ffn_conv_b[l][None, :],
                       wdn_all[l], s_len)
    return x2.reshape(b_sz, s_len, d)
```

```python
import functools
import math

import numpy as np
import jax
import jax.numpy as jnp
from jax import lax
from jax.experimental import pallas as pl
from jax.experimental.pallas import tpu as pltpu

D_MODEL = 1024
DEPTH = 4
HEAD_DIM = 64
SWA_HEADS = 8
SWA_KV_HEADS = 2
SWA_WINDOW = 128
SWA_BLOCK = 128
RNN_WIDTH = D_MODEL
RNN_BLOCKS = 16
RNN_BLOCK_WIDTH = RNN_WIDTH // RNN_BLOCKS
RNN_CONV = 4
RGLRU_C = 8.0
MOBA_HEADS = 8
MOBA_BLOCK = 256
MOBA_TOPK = 3
NUM_BUCKETS = 32
MAX_DISTANCE = 2048
D_FF = 2816
FFN_CONV = 3
RMS_EPS = 1e-6
NEG_INF = -1e30
TAKEN = -3e38
SWA_Q = SWA_HEADS * HEAD_DIM
SWA_KV = SWA_KV_HEADS * HEAD_DIM
MOBA_W = MOBA_HEADS * HEAD_DIM
SWA_GROUP = SWA_HEADS // SWA_KV_HEADS

LANES = 128
SUBLANES = 8
VMEM_LIMIT_BYTES = 56 * 1024 * 1024

MXU_DTYPE = jnp.bfloat16
SM_SCALE = HEAD_DIM ** -0.5

TM_IN = 512
TM_MERGE = 512
TM_FFN = 512
FFN_CHUNK = 256
T_RNN = 1024
SWA_STEP_BLOCKS = 4
C_RNN = 256


def _params(semantics):
    return pltpu.CompilerParams(dimension_semantics=semantics, vmem_limit_bytes=VMEM_LIMIT_BYTES)


def _dot(a, b):
    return jnp.dot(a, b, preferred_element_type=jnp.float32)


def _dot_nt(a, b):
    return lax.dot_general(a, b, (((1,), (1,)), ((), ())), preferred_element_type=jnp.float32)


def _sigmoid(x):
    return 1.0 / (1.0 + jnp.exp(-x))


def _shift_rows(u, k, halo, row8):
    H = SUBLANES
    r = pltpu.roll(u, k, 0)
    top = jnp.where(row8 < k, pltpu.roll(halo, k, 0), r[0:H, :])
    return jnp.concatenate([top, r[H:, :]], axis=0)


def _t5_bucket_np(dist):
    dist = np.maximum(dist, 0)
    max_exact = NUM_BUCKETS // 2
    ratio = np.maximum(dist, 1).astype(np.float32) / np.float32(max_exact)
    log_ratio = np.log(ratio).astype(np.float32) / np.float32(math.log(MAX_DISTANCE / max_exact))
    large = max_exact + (log_ratio * np.float32(NUM_BUCKETS - max_exact)).astype(np.int32)
    large = np.minimum(large, NUM_BUCKETS - 1)
    return np.where(dist < max_exact, dist, large).astype(np.int32)


def _moba_near_blocks():
    delta = 0
    while True:
        lo = max(delta * MOBA_BLOCK - (MOBA_BLOCK - 1), 0)
        if int(_t5_bucket_np(np.array([lo]))[0]) == NUM_BUCKETS - 1:
            return delta
        delta += 1


MOBA_NEAR = _moba_near_blocks()
MOBA_SUM_ROWS = 2 * SUBLANES
MOBA_UNROLL = 4
MOBA_AHEAD = 1
assert MOBA_NEAR <= 2 * MOBA_UNROLL and MOBA_UNROLL == 4 and MOBA_AHEAD < MOBA_UNROLL


def _bias_tile_kernel(tab_ref, bucket_ref, o_ref, *, head0):
    h = pl.program_id(0) + head0
    bucket = bucket_ref[0]
    out = jnp.full(bucket.shape, NEG_INF, jnp.float32)
    for b in range(NUM_BUCKETS):
        out = jnp.where(bucket == b, tab_ref[b, h], out)
    o_ref[0, 0] = out


def _bias_tiles_call(rel_bias, bucket_np, head0, n_heads):
    n_tiles, r, c = bucket_np.shape
    return pl.pallas_call(
        functools.partial(_bias_tile_kernel, head0=head0),
        grid=(n_heads, n_tiles),
        in_specs=[
            pl.BlockSpec(memory_space=pltpu.SMEM),
            pl.BlockSpec((1, r, c), lambda h, t: (t, 0, 0)),
        ],
        out_specs=pl.BlockSpec((1, 1, r, c), lambda h, t: (h, t, 0, 0)),
        out_shape=jax.ShapeDtypeStruct((n_heads, n_tiles, r, c), jnp.float32),
        compiler_params=_params(("arbitrary", "arbitrary")),
        name="bias_tiles",
    )(rel_bias, jnp.asarray(bucket_np, jnp.int32))


def _mod_kernel(c_ref, w_ref, b_ref, o_ref):
    c = c_ref[...]
    c_act = (c * _sigmoid(c)).astype(MXU_DTYPE)
    o_ref[0] = _dot(c_act, w_ref[0].astype(MXU_DTYPE)) + b_ref[0]


def _mod_call(c_pad, w_mod, b_mod):
    depth, d, d6 = w_mod.shape
    tn = 1536
    return pl.pallas_call(
        _mod_kernel,
        grid=(depth, d6 // tn),
        in_specs=[
            pl.BlockSpec((SUBLANES, d), lambda l, j: (0, 0)),
            pl.BlockSpec((1, d, tn), lambda l, j: (l, 0, j)),
            pl.BlockSpec((1, 1, tn), lambda l, j: (l, 0, j)),
        ],
        out_specs=pl.BlockSpec((1, SUBLANES, tn), lambda l, j: (l, 0, j)),
        out_shape=jax.ShapeDtypeStruct((depth, SUBLANES, d6), jnp.float32),
        compiler_params=_params(("arbitrary", "arbitrary")),
        name="adaln_mod",
    )(c_pad, w_mod, b_mod.reshape(depth, 1, d6))


_C_QA = 0
_C_KVA = _C_QA + SWA_Q
_C_XR = _C_KVA + 2 * SWA_KV
_C_YR = _C_XR + RNN_WIDTH
_C_KC = _C_YR + RNN_WIDTH
_C_G = _C_KC + MOBA_W
_C_END = _C_G + 3 * D_MODEL


def _head_norm_rows(z, ones_blk, gain):
    zz = z * z
    hi = zz.astype(jnp.bfloat16)
    lo = (zz - hi.astype(jnp.float32)).astype(jnp.bfloat16)
    outs = []
    for c in range(z.shape[1] // LANES):
        sl = slice(c * LANES, (c + 1) * LANES)
        ss = _dot(hi[:, sl], ones_blk) + _dot(lo[:, sl], ones_blk)
        outs.append(z[:, sl] * lax.rsqrt(ss * (1.0 / HEAD_DIM) + RMS_EPS))
    y = outs[0] if len(outs) == 1 else jnp.concatenate(outs, axis=1)
    return y * gain


def _head_norm_cols(zt, gain_col):
    w, m = zt.shape
    z3 = zt.reshape(w // HEAD_DIM, HEAD_DIM, m)
    ss = jnp.sum(z3 * z3, axis=1, keepdims=True)
    y = z3 * lax.rsqrt(ss * (1.0 / HEAD_DIM) + RMS_EPS)
    return y.reshape(w, m) * gain_col


def _inproj_kernel(x_ref, mod_ref, gain_ref, wn_ref, wt_ref, ones_ref, gq_a_ref, gk_a_ref, gk_c_ref,
                   gq_c_ref, qa_ref, ka_ref, va_ref, xr_ref, yr_ref, kc_ref, kmean_ref, g_ref,
                   qct_ref, vct_ref):
    d = D_MODEL
    x = x_ref[...]
    shift = mod_ref[0, :, 0:d]
    scale = mod_ref[0, :, d:2 * d]
    ms = jnp.mean(x * x, axis=-1, keepdims=True)
    y = x * lax.rsqrt(ms + RMS_EPS) * gain_ref[...]
    h = (y * (1.0 + scale) + shift).astype(MXU_DTYPE)
    ones_blk = ones_ref[...]

    z = _dot(h, wn_ref[:, _C_QA:_C_KVA])
    qa_ref[...] = (_head_norm_rows(z, ones_blk, gq_a_ref[...]) * SM_SCALE).astype(qa_ref.dtype)

    z = _dot(h, wn_ref[:, _C_KVA:_C_XR])
    ka_ref[...] = _head_norm_rows(z[:, :SWA_KV], ones_blk, gk_a_ref[...]).astype(ka_ref.dtype)
    va_ref[...] = z[:, SWA_KV:].astype(va_ref.dtype)

    xr_ref[...] = _dot(h, wn_ref[:, _C_XR:_C_YR])
    yr_ref[...] = _dot(h, wn_ref[:, _C_YR:_C_KC])

    z = _dot(h, wn_ref[:, _C_KC:_C_G])
    kc = _head_norm_rows(z, ones_blk, gk_c_ref[...])
    kc_ref[...] = kc.astype(kc_ref.dtype)
    tm = x.shape[0]
    nblk = tm // MOBA_BLOCK
    kmean_ref[0] = jnp.mean(kc.reshape(nblk, MOBA_BLOCK, MOBA_W), axis=1)

    for c in range(3 * D_MODEL // 512):
        g_ref[:, c * 512:(c + 1) * 512] = _dot(h, wn_ref[:, _C_G + c * 512:_C_G + (c + 1) * 512])

    zt = _dot_nt(wt_ref[0:MOBA_W, :], h)
    qct = (_head_norm_cols(zt, gq_c_ref[...]) * SM_SCALE).astype(qct_ref.dtype)
    zt = _dot_nt(wt_ref[MOBA_W:2 * MOBA_W, :], h)
    vct = zt.astype(vct_ref.dtype)
    for r in range(nblk):
        qct_ref[0, r] = qct[:, r * MOBA_BLOCK:(r + 1) * MOBA_BLOCK]
        vct_ref[0, r] = vct[:, r * MOBA_BLOCK:(r + 1) * MOBA_BLOCK]


def _inproj_call(x2, mods_l, gain, wn, wt, ones_blk, gq_a, gk_a, gk_c, gq_c, b_sz, s_len):
    n, d = x2.shape
    tm = min(TM_IN, s_len)
    tpb = s_len // tm
    nblk = tm // MOBA_BLOCK
    nb = s_len // MOBA_BLOCK
    row = lambda i: (i, 0)
    const2 = lambda i: (0, 0)
    bf = MXU_DTYPE
    out_shape = (
        jax.ShapeDtypeStruct((n, SWA_Q), bf),
        jax.ShapeDtypeStruct((n, SWA_KV), bf),
        jax.ShapeDtypeStruct((n, SWA_KV), bf),
        jax.ShapeDtypeStruct((n, RNN_WIDTH), jnp.float32),
        jax.ShapeDtypeStruct((n, RNN_WIDTH), jnp.float32),
        jax.ShapeDtypeStruct((n, MOBA_W), bf),
        jax.ShapeDtypeStruct((n // tm, nblk, MOBA_W), jnp.float32),
        jax.ShapeDtypeStruct((n, 3 * D_MODEL), jnp.float32),
        jax.ShapeDtypeStruct((b_sz, nb, MOBA_W, MOBA_BLOCK), bf),
        jax.ShapeDtypeStruct((b_sz, nb, MOBA_W, MOBA_BLOCK), bf),
    )
    blk4 = pl.BlockSpec((1, nblk, MOBA_W, MOBA_BLOCK), lambda i: (i // tpb, i % tpb, 0, 0))
    out_specs = (
        pl.BlockSpec((tm, SWA_Q), row),
        pl.BlockSpec((tm, SWA_KV), row),
        pl.BlockSpec((tm, SWA_KV), row),
        pl.BlockSpec((tm, RNN_WIDTH), row),
        pl.BlockSpec((tm, RNN_WIDTH), row),
        pl.BlockSpec((tm, MOBA_W), row),
        pl.BlockSpec((1, nblk, MOBA_W), lambda i: (i, 0, 0)),
        pl.BlockSpec((tm, 3 * D_MODEL), row),
        blk4,
        blk4,
    )
    in_specs = [
        pl.BlockSpec((tm, d), row),
        pl.BlockSpec((1, 1, 6 * d), lambda i: (i // tpb, 0, 0)),
        pl.BlockSpec((1, d), const2),
        pl.BlockSpec(wn.shape, const2, pipeline_mode=pl.Buffered(1)),
        pl.BlockSpec(wt.shape, const2, pipeline_mode=pl.Buffered(1)),
        pl.BlockSpec(ones_blk.shape, const2),
        pl.BlockSpec(gq_a.shape, const2),
        pl.BlockSpec(gk_a.shape, const2),
        pl.BlockSpec(gk_c.shape, const2),
        pl.BlockSpec(gq_c.shape, const2),
    ]
    return pl.pallas_call(
        _inproj_kernel,
        grid=(n // tm,),
        in_specs=in_specs,
        out_specs=out_specs,
        out_shape=out_shape,
        compiler_params=_params(("arbitrary",)),
        name="inproj",
    )(x2, mods_l, gain, wn, wt, ones_blk, gq_a, gk_a, gk_c, gq_c)


def _swa_kernel(sink_ref, q_ref, kp_ref, ko_ref, vp_ref, vo_ref, bias_ref, o_ref, s_ref, p_ref):
    i = pl.program_id(1)
    L = SWA_BLOCK
    qi = lax.broadcasted_iota(jnp.int32, (L, 2 * L), 0)
    sj = lax.broadcasted_iota(jnp.int32, (L, 2 * L), 1)
    diff = qi + L - sj
    in_window = (diff >= 0) & (diff < SWA_WINDOW)
    lane = lax.broadcasted_iota(jnp.int32, (L, LANES), 1)
    low = lane < HEAD_DIM
    zero = jnp.zeros((), q_ref.dtype)
    for j in range(q_ref.shape[1] // L):
        rows = slice(j * L, (j + 1) * L)
        if j == 0:
            k_prev, v_prev = kp_ref[0], vp_ref[0]
            valid = in_window & ((i > 0) | (sj >= L))
        else:
            before = slice((j - 1) * L, j * L)
            k_prev, v_prev = ko_ref[0, before, :], vo_ref[0, before, :]
            valid = in_window
        kcat = jnp.concatenate([k_prev, ko_ref[0, rows, :]], axis=0)
        vcat = jnp.concatenate([v_prev, vo_ref[0, rows, :]], axis=0)
        for c in range(SWA_Q // LANES):
            q2 = q_ref[0, rows, c * LANES:(c + 1) * LANES]
            for g in range(SWA_KV_HEADS):
                head = g * SWA_GROUP + c
                qm = jnp.where(low if g == 0 else ~low, q2, zero)
                s_ref[j, head] = jnp.where(valid, _dot_nt(qm, kcat) + bias_ref[head], NEG_INF)
        for head in range(SWA_HEADS):
            s = s_ref[j, head]
            sink = sink_ref[head]
            m = jnp.maximum(jnp.max(s, axis=-1, keepdims=True), sink)
            p = jnp.exp(s - m)
            denom = jnp.sum(p, axis=-1, keepdims=True) + jnp.exp(sink - m)
            p_ref[j, head] = (p / denom).astype(p_ref.dtype)
        for c in range(SWA_Q // LANES):
            halves = [_dot(p_ref[j, g * SWA_GROUP + c], vcat) for g in range(SWA_KV_HEADS)]
            o_ref[0, rows, c * LANES:(c + 1) * LANES] = jnp.where(low, halves[0], halves[1]).astype(o_ref.dtype)


def _swa_call(qa, ka, va, sinks_l, bias_a, b_sz, s_len):
    L = SWA_BLOCK
    nb = min(SWA_STEP_BLOCKS, s_len // L)
    q3 = qa.reshape(b_sz, s_len, SWA_Q)
    k3 = ka.reshape(b_sz, s_len, SWA_KV)
    v3 = va.reshape(b_sz, s_len, SWA_KV)
    prev = lambda b, i: (b, jnp.maximum(i * nb - 1, 0), 0)
    own = lambda b, i: (b, i, 0)
    out = pl.pallas_call(
        _swa_kernel,
        grid=(b_sz, s_len // (nb * L)),
        in_specs=[
            pl.BlockSpec(memory_space=pltpu.SMEM),
            pl.BlockSpec((1, nb * L, SWA_Q), own),
            pl.BlockSpec((1, L, SWA_KV), prev),
            pl.BlockSpec((1, nb * L, SWA_KV), own),
            pl.BlockSpec((1, L, SWA_KV), prev),
            pl.BlockSpec((1, nb * L, SWA_KV), own),
            pl.BlockSpec((SWA_HEADS, L, 2 * L), lambda b, i: (0, 0, 0)),
        ],
        out_specs=pl.BlockSpec((1, nb * L, SWA_Q), own),
        out_shape=jax.ShapeDtypeStruct((b_sz, s_len, SWA_Q), MXU_DTYPE),
        scratch_shapes=[
            pltpu.VMEM((nb, SWA_HEADS, L, 2 * L), jnp.float32),
            pltpu.VMEM((nb, SWA_HEADS, L, 2 * L), MXU_DTYPE),
        ],
        compiler_params=_params(("arbitrary", "arbitrary")),
        name="swa",
    )(sinks_l, q3, k3, k3, v3, v3, bias_a)
    return out.reshape(b_sz * s_len, SWA_Q)


def _gelu_tanh(x):
    return 0.5 * x * (1.0 + jnp.tanh(math.sqrt(2.0 / math.pi) * (x + 0.044715 * (x * x * x))))


def _rnn_kernel(xr_ref, yr_ref, cw_ref, cb_ref, wg_ref, bg_ref, lam_ref, o_ref, xbuf_ref, h_ref):
    t = pl.program_id(2)
    T = xr_ref.shape[1]
    C = xr_ref.shape[2]
    H = SUBLANES

    @pl.when(t == 0)
    def _():
        xbuf_ref[...] = jnp.zeros(xbuf_ref.shape, jnp.float32)
        h_ref[...] = jnp.zeros(h_ref.shape, jnp.float32)

    x = xr_ref[0]
    halo = xbuf_ref[...]
    row8 = lax.broadcasted_iota(jnp.int32, (H, C), 0)
    xc = x * cw_ref[RNN_CONV - 1:RNN_CONV, :] + cb_ref[...]
    for j in range(1, RNN_CONV):
        xc = xc + _shift_rows(x, j, halo, row8) * cw_ref[RNN_CONV - 1 - j:RNN_CONV - j, :]
    xbuf_ref[...] = x[T - H:T, :]

    gates = _dot(xc.astype(MXU_DTYPE), wg_ref[0]) + bg_ref[0]
    r = _sigmoid(gates[:, :C])
    i = _sigmoid(gates[:, C:])
    nlam = -lam_ref[...]
    softplus = jnp.maximum(nlam, 0.0) + jnp.log1p(jnp.exp(-jnp.abs(nlam)))
    log_a = (-RGLRU_C) * r * softplus
    a = jnp.exp(log_a)
    mult = jnp.sqrt(jnp.tanh(-log_a) * (1.0 + a * a))
    row = lax.broadcasted_iota(jnp.int32, (T, C), 0)
    mult = jnp.where((row == 0) & (t == 0), 1.0, mult)
    u = mult * (i * xc)

    G = T // H
    a = a.reshape(G, H, C)
    u = u.reshape(G, H, C)
    sub = lax.broadcasted_iota(jnp.int32, (G, H, C), 1)
    step = 1
    while step < H:
        keep = sub >= step
        a_prev = jnp.where(keep, pltpu.roll(a, step, 1), 1.0)
        u_prev = jnp.where(keep, pltpu.roll(u, step, 1), 0.0)
        u = a * u_prev + u
        a = a * a_prev
        step *= 2
    carry = h_ref[0:1, :]
    groups = []
    for g in range(G):
        hg = u[g] + a[g] * carry
        groups.append(hg)
        carry = hg[H - 1:H, :]
    h_ref[0:1, :] = carry
    hs = jnp.concatenate(groups, axis=0)
    o_ref[0] = (hs * _gelu_tanh(yr_ref[0])).astype(o_ref.dtype)


def _rnn_call(xr, yr, conv_w, conv_b, wg, bg, lam, b_sz, s_len):
    T = min(T_RNN, s_len)
    C = C_RNN
    nc = RNN_WIDTH // C
    x3 = xr.reshape(b_sz, s_len, RNN_WIDTH)
    y3 = yr.reshape(b_sz, s_len, RNN_WIDTH)
    tile = lambda b, c, t: (b, t, c)
    chan = lambda b, c, t: (0, c)
    out = pl.pallas_call(
        _rnn_kernel,
        grid=(b_sz, nc, s_len // T),
        in_specs=[
            pl.BlockSpec((1, T, C), tile),
            pl.BlockSpec((1, T, C), tile),
            pl.BlockSpec((RNN_CONV, C), chan),
            pl.BlockSpec((1, C), chan),
            pl.BlockSpec((1, C, 2 * C), lambda b, c, t: (c, 0, 0)),
            pl.BlockSpec((1, 1, 2 * C), lambda b, c, t: (c, 0, 0)),
            pl.BlockSpec((1, C), chan),
        ],
        out_specs=pl.BlockSpec((1, T, C), tile),
        out_shape=jax.ShapeDtypeStruct((b_sz, s_len, RNN_WIDTH), MXU_DTYPE),
        scratch_shapes=[
            pltpu.VMEM((SUBLANES, C), jnp.float32),
            pltpu.VMEM((SUBLANES, C), jnp.float32),
        ],
        compiler_params=_params(("arbitrary", "arbitrary", "arbitrary")),
        name="rglru",
    )(x3, y3, conv_w, conv_b, wg, bg, lam)
    return out.reshape(b_sz * s_len, RNN_WIDTH)


def _split_heads(qt):
    rows = lax.broadcasted_iota(jnp.int32, qt.shape, 0)
    zero = jnp.zeros((), qt.dtype)
    return [jnp.where(rows < HEAD_DIM, qt, zero), jnp.where(rows >= HEAD_DIM, qt, zero)]


def _moba_kernel(far_ref, qt_ref, k_ref, vt_ref, kmean_ref, bias_ref, o_ref, sel_ref, qs_ref, s0_ref, s1_ref,
                 s2_ref, s3_ref, mx_ref, pva_ref, pvb_ref, ala_ref, alb_ref, m_ref, acc_ref):
    n = pl.program_id(2)
    MB = MOBA_BLOCK
    nb = kmean_ref.shape[1]
    qts = _split_heads(qt_ref[0, 0])

    kmean = kmean_ref[0].astype(MXU_DTYPE)
    blk = lax.broadcasted_iota(jnp.int32, (nb, MB), 0).astype(jnp.float32)
    n_f = n.astype(jnp.float32)
    past = blk < n_f
    for e in range(2):
        gate = _dot(kmean, qts[e])
        gate = jnp.where(past, gate, NEG_INF)
        sel = jnp.zeros((nb, MB), jnp.float32)
        for _ in range(MOBA_TOPK):
            mx = jnp.max(gate, axis=0, keepdims=True)
            first = jnp.min(jnp.where(gate == mx, blk, float(nb)), axis=0, keepdims=True)
            hit = blk == first
            sel = jnp.where(hit, 1.0, sel)
            gate = jnp.where(hit, TAKEN, gate)
        sel_ref[e] = jnp.where(blk == n_f, 1.0, jnp.where(past, sel, 0.0))

    for e in range(2):
        qs_ref[e] = qts[e]
    m_ref[...] = jnp.full(m_ref.shape, NEG_INF, jnp.float32)
    acc_ref[...] = jnp.zeros(acc_ref.shape, jnp.float32)
    pvb_ref[...] = jnp.zeros(pvb_ref.shape, jnp.float32)
    alb_ref[...] = jnp.ones(alb_ref.shape, jnp.float32)
    ones_rows = jnp.ones((MOBA_SUM_ROWS, MB), MXU_DTYPE)

    def block_of(t):
        return jnp.clip(n - t, 0, n)

    def accumulate(pv_ref, al_ref):
        for e in range(2):
            acc_ref[e] = al_ref[e] * acc_ref[e] + pv_ref[e]

    far_bias = [far_ref[2 * pl.program_id(1) + e] for e in range(2)]

    s_bufs = (s0_ref, s1_ref, s2_ref, s3_ref)

    def scores(t, slot, far):
        m = block_of(t)
        k_m = k_ref[0, pl.ds(pl.multiple_of(m * MB, MB), MB), :]
        for e in range(2):
            s = _dot(k_m, qs_ref[e])
            if not far:
                s = s + bias_ref[e, t]
            s_bufs[slot][e] = s
            part = s[0:SUBLANES, :]
            for g in range(1, MB // SUBLANES):
                part = jnp.maximum(part, s[g * SUBLANES:(g + 1) * SUBLANES, :])
            mx_ref[slot, e] = part

    def softmax(t, slot, pv_ref, al_ref, far):
        m = block_of(t)
        live = t <= n
        vt_m = vt_ref[0, m]
        for e in range(2):
            s = s_bufs[slot][e]
            chosen = (sel_ref[e, pl.ds(m, 1), :] > 0.0) & live
            mx = jnp.max(mx_ref[slot, e], axis=0, keepdims=True)
            if far:
                mx = mx + far_bias[e]
            mx = jnp.where(chosen, mx, NEG_INF)
            m_old = m_ref[e]
            m_new = jnp.maximum(m_old, mx)
            al_ref[e] = jnp.exp(m_old - m_new)
            sub = m_new - far_bias[e] if far else m_new
            p = jnp.exp(s - jnp.where(chosen, sub, -NEG_INF)).astype(MXU_DTYPE)
            m_ref[e] = m_new
            lhs = jnp.concatenate([vt_m[e * HEAD_DIM:(e + 1) * HEAD_DIM, :], ones_rows], axis=0)
            pv_ref[e] = _dot(lhs, p)

    pv_bufs = ((pva_ref, ala_ref), (pvb_ref, alb_ref))

    def region(t, slot, far, far_ahead):
        pv_cur, al_cur = pv_bufs[slot % 2]
        pv_oth, al_oth = pv_bufs[1 - slot % 2]
        scores(t + MOBA_AHEAD, (slot + MOBA_AHEAD) % MOBA_UNROLL, far_ahead)
        accumulate(pv_oth, al_oth)
        softmax(t, slot, pv_cur, al_cur, far)

    def near_regions(t_lo, t_hi):
        for t in range(t_lo, t_hi):
            region(t, t % MOBA_UNROLL, t >= MOBA_NEAR, t + MOBA_AHEAD >= MOBA_NEAR)

    near_steps = 2 * MOBA_UNROLL
    for t in range(MOBA_AHEAD):
        scores(t, t, False)
    near_regions(0, MOBA_UNROLL)

    @pl.when(n >= MOBA_UNROLL)
    def _():
        near_regions(MOBA_UNROLL, near_steps)

    def far_regions(t0, slot0, count):
        for r in range(count):
            region(t0 + r, slot0 + r, True, True)

    n_far = jnp.maximum(n + 1 - near_steps, 0)
    n_quads = (n_far + 1) // MOBA_UNROLL
    n_pairs = jnp.where(n_far > MOBA_UNROLL * n_quads, 1, 0)

    def quad_body(i, carry):
        far_regions(near_steps + MOBA_UNROLL * i, 0, MOBA_UNROLL)
        return carry

    lax.fori_loop(0, n_quads, quad_body, 0)
    t_pairs = near_steps + MOBA_UNROLL * n_quads

    @pl.when(n_pairs > 0)
    def _():
        far_regions(t_pairs, 0, 2)

    accumulate(pvb_ref, alb_ref)

    top = acc_ref[0, 0:HEAD_DIM, :] / acc_ref[0, HEAD_DIM:HEAD_DIM + 1, :]
    bot = acc_ref[1, 0:HEAD_DIM, :] / acc_ref[1, HEAD_DIM:HEAD_DIM + 1, :]
    o_ref[0] = jnp.concatenate([top, bot], axis=0).T.astype(o_ref.dtype)


def _moba_call(qct, kc, vct, kmean, bias_c, far_c, b_sz, s_len):
    MB = MOBA_BLOCK
    nb = s_len // MB
    k3 = kc.reshape(b_sz, s_len, MOBA_W)
    km3 = kmean.reshape(b_sz, nb, MOBA_W)
    npair = MOBA_W // LANES
    out = pl.pallas_call(
        _moba_kernel,
        grid=(b_sz, npair, nb),
        in_specs=[
            pl.BlockSpec(memory_space=pltpu.SMEM),
            pl.BlockSpec((1, 1, LANES, MB), lambda b, p, n: (b, n, p, 0)),
            pl.BlockSpec((1, s_len, LANES), lambda b, p, n: (b, 0, p)),
            pl.BlockSpec((1, nb, LANES, MB), lambda b, p, n: (b, 0, p, 0)),
            pl.BlockSpec((1, nb, LANES), lambda b, p, n: (b, 0, p)),
            pl.BlockSpec((2, MOBA_NEAR, MB, MB), lambda b, p, n: (p, 0, 0, 0)),
        ],
        out_specs=pl.BlockSpec((1, MB, LANES), lambda b, p, n: (b, n, p)),
        out_shape=jax.ShapeDtypeStruct((b_sz, s_len, MOBA_W), MXU_DTYPE),
        scratch_shapes=[
            pltpu.VMEM((2, nb, MB), jnp.float32),
            pltpu.VMEM((2, LANES, MB), MXU_DTYPE),
            pltpu.VMEM((2, MB, MB), jnp.float32),
            pltpu.VMEM((2, MB, MB), jnp.float32),
            pltpu.VMEM((2, MB, MB), jnp.float32),
            pltpu.VMEM((2, MB, MB), jnp.float32),
            pltpu.VMEM((MOBA_UNROLL, 2, SUBLANES, MB), jnp.float32),
            pltpu.VMEM((2, HEAD_DIM + MOBA_SUM_ROWS, MB), jnp.float32),
            pltpu.VMEM((2, HEAD_DIM + MOBA_SUM_ROWS, MB), jnp.float32),
            pltpu.VMEM((2, 1, MB), jnp.float32),
            pltpu.VMEM((2, 1, MB), jnp.float32),
            pltpu.VMEM((2, 1, MB), jnp.float32),
            pltpu.VMEM((2, HEAD_DIM + MOBA_SUM_ROWS, MB), jnp.float32),
        ],
        compiler_params=_params(("arbitrary", "arbitrary", "arbitrary")),
        name="moba",
    )(far_c, qct, k3, vct, km3, bias_c)
    return out.reshape(b_sz * s_len, MOBA_W)


def _merge_kernel(x_ref, oa_ref, ob_ref, oc_ref, g_ref, mod_ref, wa_ref, wb_ref, wc_ref, wo_ref, o_ref):
    d = D_MODEL
    merged = _sigmoid(g_ref[:, 0:d]) * _dot(oa_ref[...], wa_ref[...])
    merged = merged + _sigmoid(g_ref[:, d:2 * d]) * _dot(ob_ref[...], wb_ref[...])
    merged = merged + _sigmoid(g_ref[:, 2 * d:3 * d]) * _dot(oc_ref[...], wc_ref[...])
    y = _dot(merged.astype(MXU_DTYPE), wo_ref[...])
    gate_m = mod_ref[0, :, 2 * d:3 * d]
    o_ref[...] = x_ref[...] + gate_m * y


def _merge_call(x2, o_a, o_b, o_c, g_logits, mods_l, wa, wb, wc, wo, s_len):
    n, d = x2.shape
    tm = min(TM_MERGE, s_len)
    tpb = s_len // tm
    row = lambda i: (i, 0)
    const2 = lambda i: (0, 0)
    return pl.pallas_call(
        _merge_kernel,
        grid=(n // tm,),
        in_specs=[
            pl.BlockSpec((tm, d), row),
            pl.BlockSpec((tm, SWA_Q), row),
            pl.BlockSpec((tm, RNN_WIDTH), row),
            pl.BlockSpec((tm, MOBA_W), row),
            pl.BlockSpec((tm, 3 * d), row),
            pl.BlockSpec((1, 1, 6 * d), lambda i: (i // tpb, 0, 0)),
            pl.BlockSpec(wa.shape, const2),
            pl.BlockSpec(wb.shape, const2),
            pl.BlockSpec(wc.shape, const2),
            pl.BlockSpec(wo.shape, const2),
        ],
        out_specs=pl.BlockSpec((tm, d), row),
        out_shape=jax.ShapeDtypeStruct((n, d), jnp.float32),
        compiler_params=_params(("arbitrary",)),
        name="merge",
    )(x2, o_a, o_b, o_c, g_logits, mods_l, wa, wb, wc, wo)


def _ffn_kernel(x_ref, mod_ref, gain_ref, wu_ref, cw_ref, cb_ref, wd_ref, o_ref, h_ref, act_ref, halo_ref,
                *, tiles_per_seq):
    i = pl.program_id(0)
    d = D_MODEL
    tm = x_ref.shape[0]
    H = SUBLANES
    CH = FFN_CHUNK

    @pl.when(i == 0)
    def _():
        halo_ref[...] = jnp.zeros(halo_ref.shape, jnp.float32)

    x = x_ref[...]
    shift = mod_ref[0, :, 3 * d:4 * d]
    scale = mod_ref[0, :, 4 * d:5 * d]
    ms = jnp.mean(x * x, axis=-1, keepdims=True)
    y = x * lax.rsqrt(ms + RMS_EPS) * gain_ref[...]
    h_ref[...] = (y * (1.0 + scale) + shift).astype(h_ref.dtype)

    seq_start = (i % tiles_per_seq) == 0
    row8 = lax.broadcasted_iota(jnp.int32, (H, CH), 0)

    def conv(col):
        cols = slice(col, col + CH)
        u = _dot(h_ref[...], wu_ref[:, cols])
        halo = jnp.where(seq_start, 0.0, halo_ref[:, cols])
        halo_ref[:, cols] = u[tm - H:tm, :]
        out = u * cw_ref[FFN_CONV - 1:FFN_CONV, cols] + cb_ref[:, cols]
        for k in range(1, FFN_CONV):
            out = out + _shift_rows(u, k, halo, row8) * cw_ref[FFN_CONV - 1 - k:FFN_CONV - k, cols]
        return out

    for c in range(D_FF // CH):
        ug = conv(c * CH)
        uv = conv(D_FF + c * CH)
        act_ref[:, c * CH:(c + 1) * CH] = (ug * _sigmoid(ug) * uv).astype(act_ref.dtype)

    gate_f = mod_ref[0, :, 5 * d:6 * d]
    o_ref[...] = x_ref[...] + gate_f * _dot(act_ref[...], wd_ref[...])


def _ffn_call(x2, mods_l, gain, w_up, conv_w, conv_b, w_down, s_len):
    n, d = x2.shape
    tm = min(TM_FFN, s_len)
    tpb = s_len // tm
    row = lambda i: (i, 0)
    const2 = lambda i: (0, 0)
    kern = functools.partial(_ffn_kernel, tiles_per_seq=tpb)
    return pl.pallas_call(
        kern,
        grid=(n // tm,),
        in_specs=[
            pl.BlockSpec((tm, d), row),
            pl.BlockSpec((1, 1, 6 * d), lambda i: (i // tpb, 0, 0)),
            pl.BlockSpec((1, d), const2),
            pl.BlockSpec(w_up.shape, const2, pipeline_mode=pl.Buffered(1)),
            pl.BlockSpec(conv_w.shape, const2),
            pl.BlockSpec(conv_b.shape, const2),
            pl.BlockSpec(w_down.shape, const2, pipeline_mode=pl.Buffered(1)),
        ],
        out_specs=pl.BlockSpec((tm, d), row),
        out_shape=jax.ShapeDtypeStruct((n, d), jnp.float32),
        scratch_shapes=[
            pltpu.VMEM((tm, d), MXU_DTYPE),
            pltpu.VMEM((tm, D_FF), MXU_DTYPE),
            pltpu.VMEM((SUBLANES, 2 * D_FF), jnp.float32),
        ],
        compiler_params=_params(("arbitrary",)),
        name="ffn",
    )(x2, mods_l, gain, w_up, conv_w, conv_b, w_down)


def _swa_head_order():
    order = []
    for c in range(SWA_Q // LANES):
        for g in range(SWA_KV_HEADS):
            order.append(g * SWA_GROUP + c)
    return order


def _bias_tables(rel_bias):
    L = SWA_BLOCK
    qi = np.arange(L)[:, None]
    sj = np.arange(2 * L)[None, :]
    bucket_a = _t5_bucket_np(qi + L - sj)[None]
    tab_a = _bias_tiles_call(rel_bias, bucket_a, 0, SWA_HEADS)[:, 0]
    MB = MOBA_BLOCK
    kj = np.arange(MB)[:, None]
    qq = np.arange(MB)[None, :]
    deltas = np.arange(MOBA_NEAR)[:, None, None]
    bucket_c = _t5_bucket_np(deltas * MB + qq[None] - kj[None])
    bucket_c = np.where((deltas > 0) | (kj <= qq)[None], bucket_c, NUM_BUCKETS)
    tab_c = _bias_tiles_call(rel_bias, bucket_c, SWA_HEADS, MOBA_HEADS)
    far_c = rel_bias[NUM_BUCKETS - 1, SWA_HEADS:]
    return tab_a, tab_c, far_c


def kernel(x, c, w_mod, b_mod, norm_mix, norm_ffn, w_in, qnorm_a, knorm_a, sinks, rnn_conv_w, rnn_conv_b,
           rnn_gate_a_w, rnn_gate_a_b, rnn_gate_x_w, rnn_gate_x_b, rnn_lambda, qnorm_c, knorm_c, rel_bias,
           w_branch, w_out, w_up, ffn_conv_w, ffn_conv_b, w_down):
    b_sz, s_len, d = x.shape
    depth = w_in.shape[0]
    assert d == D_MODEL and s_len % MOBA_BLOCK == 0 and b_sz <= SUBLANES
    n = b_sz * s_len
    bf = MXU_DTYPE
    f32 = jnp.float32

    c_pad = jnp.zeros((SUBLANES, d), f32).at[:b_sz].set(c)
    mods = _mod_call(c_pad, w_mod, b_mod)[:, :b_sz]
    mods = mods.reshape(depth, b_sz, 1, 6 * d)

    offs = np.cumsum([0, SWA_Q, SWA_KV, SWA_KV, RNN_WIDTH, RNN_WIDTH, MOBA_W, MOBA_W, MOBA_W, 3 * d])
    o_qa, o_ka, o_va, o_xr, o_yr, o_qc, o_kc, o_vc, o_g, o_end = [int(v) for v in offs]
    head_order = _swa_head_order()
    w_in_b = w_in.astype(bf)
    wn_all = jnp.concatenate(
        [w_in_b[:, :, o_qa + h * HEAD_DIM:o_qa + (h + 1) * HEAD_DIM] for h in head_order]
        + [w_in_b[:, :, o_ka:o_qc], w_in_b[:, :, o_kc:o_vc], w_in_b[:, :, o_g:o_end]], axis=-1)
    wt_all = jnp.swapaxes(jnp.concatenate([w_in_b[:, :, o_qc:o_kc], w_in_b[:, :, o_vc:o_g]], axis=-1), 1, 2)
    lane_head = np.arange(LANES) // HEAD_DIM
    ones_blk = jnp.asarray(lane_head[:, None] == lane_head[None, :], bf)

    wa_all = jnp.concatenate(
        [w_branch[:, h * HEAD_DIM:(h + 1) * HEAD_DIM] for h in head_order], axis=1).astype(bf)
    wb_all = w_branch[:, SWA_Q:SWA_Q + RNN_WIDTH].astype(bf)
    wc_all = w_branch[:, SWA_Q + RNN_WIDTH:].astype(bf)
    wo_all = w_out.astype(bf)
    wup_all = w_up.astype(bf)
    wdn_all = w_down.astype(bf)

    per = C_RNN // RNN_BLOCK_WIDTH
    nct = RNN_WIDTH // C_RNN
    eye = jnp.eye(per, dtype=f32)

    def blockdiag(w):
        w5 = w.reshape(depth, nct, per, RNN_BLOCK_WIDTH, RNN_BLOCK_WIDTH)
        full = jnp.einsum('lcpij,pq->lcpiqj', w5, eye)
        return full.reshape(depth, nct, C_RNN, C_RNN)

    wg_all = jnp.concatenate([blockdiag(rnn_gate_a_w), blockdiag(rnn_gate_x_w)], axis=-1).astype(bf)
    bg_all = jnp.concatenate([rnn_gate_a_b.reshape(depth, nct, 1, C_RNN),
                              rnn_gate_x_b.reshape(depth, nct, 1, C_RNN)], axis=-1)

    tab_a, tab_c, far_c = _bias_tables(rel_bias)

    x2 = x.reshape(n, d)
    for l in range(depth):
        gq_a = jnp.tile(qnorm_a[l], SWA_Q // HEAD_DIM)[None, :]
        gk_a = jnp.tile(knorm_a[l], SWA_KV // HEAD_DIM)[None, :]
        gk_c = jnp.tile(knorm_c[l], MOBA_W // HEAD_DIM)[None, :]
        gq_c = jnp.tile(qnorm_c[l], MOBA_W // HEAD_DIM)[:, None]
        qa, ka, va, xr, yr, kc, kmean, g_logits, qct, vct = _inproj_call(
            x2, mods[l], norm_mix[l][None, :], wn_all[l], wt_all[l], ones_blk, gq_a, gk_a, gk_c, gq_c,
            b_sz, s_len)
        o_a = _swa_call(qa, ka, va, sinks[l], tab_a, b_sz, s_len)
        o_b = _rnn_call(xr, yr, rnn_conv_w[l], rnn_conv_b[l][None, :], wg_all[l], bg_all[l],
                        rnn_lambda[l][None, :], b_sz, s_len)
        o_c = _moba_call(qct, kc, vct, kmean, tab_c, far_c, b_sz, s_len)
        x2 = _merge_call(x2, o_a, o_b, o_c, g_logits, mods[l], wa_all[l], wb_all[l], wc_all[l], wo_all[l],
                         s_len)
        x2 = _ffn_call(x2, mods[l], norm_ffn[l][None, :], wup_all[l], ffn_conv_w[l], ffn_conv_b[l][None, :],
                       wdn_all[l], s_len)
    return x2.reshape(b_sz, s_len, d)
```

```python
import functools
import math

import numpy as np
import jax
import jax.numpy as jnp
from jax import lax
from jax.experimental import pallas as pl
from jax.experimental.pallas import tpu as pltpu

D_MODEL = 1024
DEPTH = 4
HEAD_DIM = 64
SWA_HEADS = 8
SWA_KV_HEADS = 2
SWA_WINDOW = 128
SWA_BLOCK = 128
RNN_WIDTH = D_MODEL
RNN_BLOCKS = 16
RNN_BLOCK_WIDTH = RNN_WIDTH // RNN_BLOCKS
RNN_CONV = 4
RGLRU_C = 8.0
MOBA_HEADS = 8
MOBA_BLOCK = 256
MOBA_TOPK = 3
NUM_BUCKETS = 32
MAX_DISTANCE = 2048
D_FF = 2816
FFN_CONV = 3
RMS_EPS = 1e-6
NEG_INF = -1e30
TAKEN = -3e38
SWA_Q = SWA_HEADS * HEAD_DIM
SWA_KV = SWA_KV_HEADS * HEAD_DIM
MOBA_W = MOBA_HEADS * HEAD_DIM
SWA_GROUP = SWA_HEADS // SWA_KV_HEADS

LANES = 128
SUBLANES = 8
VMEM_LIMIT_BYTES = 56 * 1024 * 1024

MXU_DTYPE = jnp.bfloat16
SM_SCALE = HEAD_DIM ** -0.5

TM_IN = 512
TM_MERGE = 512
TM_FFN = 512
FFN_CHUNK = 256
T_RNN = 1024
SWA_STEP_BLOCKS = 4
C_RNN = 256


def _params(semantics):
    return pltpu.CompilerParams(dimension_semantics=semantics, vmem_limit_bytes=VMEM_LIMIT_BYTES)


def _dot(a, b):
    return jnp.dot(a, b, preferred_element_type=jnp.float32)


def _dot_nt(a, b):
    return lax.dot_general(a, b, (((1,), (1,)), ((), ())), preferred_element_type=jnp.float32)


def _sigmoid(x):
    return 1.0 / (1.0 + jnp.exp(-x))


def _shift_rows(u, k, halo, row8):
    H = SUBLANES
    r = pltpu.roll(u, k, 0)
    top = jnp.where(row8 < k, pltpu.roll(halo, k, 0), r[0:H, :])
    return jnp.concatenate([top, r[H:, :]], axis=0)


def _t5_bucket_np(dist):
    dist = np.maximum(dist, 0)
    max_exact = NUM_BUCKETS // 2
    ratio = np.maximum(dist, 1).astype(np.float32) / np.float32(max_exact)
    log_ratio = np.log(ratio).astype(np.float32) / np.float32(math.log(MAX_DISTANCE / max_exact))
    large = max_exact + (log_ratio * np.float32(NUM_BUCKETS - max_exact)).astype(np.int32)
    large = np.minimum(large, NUM_BUCKETS - 1)
    return np.where(dist < max_exact, dist, large).astype(np.int32)


def _moba_near_blocks():
    delta = 0
    while True:
        lo = max(delta * MOBA_BLOCK - (MOBA_BLOCK - 1), 0)
        if int(_t5_bucket_np(np.array([lo]))[0]) == NUM_BUCKETS - 1:
            return delta
        delta += 1


MOBA_NEAR = _moba_near_blocks()
MOBA_SUM_ROWS = 2 * SUBLANES
MOBA_UNROLL = 4
MOBA_AHEAD = 1
assert MOBA_NEAR <= 2 * MOBA_UNROLL and MOBA_UNROLL == 4 and MOBA_AHEAD < MOBA_UNROLL


def _bias_tile_kernel(tab_ref, bucket_ref, o_ref, *, head0):
    h = pl.program_id(0) + head0
    bucket = bucket_ref[0]
    out = jnp.full(bucket.shape, NEG_INF, jnp.float32)
    for b in range(NUM_BUCKETS):
        out = jnp.where(bucket == b, tab_ref[b, h], out)
    o_ref[0, 0] = out


def _bias_tiles_call(rel_bias, bucket_np, head0, n_heads):
    n_tiles, r, c = bucket_np.shape
    return pl.pallas_call(
        functools.partial(_bias_tile_kernel, head0=head0),
        grid=(n_heads, n_tiles),
        in_specs=[
            pl.BlockSpec(memory_space=pltpu.SMEM),
            pl.BlockSpec((1, r, c), lambda h, t: (t, 0, 0)),
        ],
        out_specs=pl.BlockSpec((1, 1, r, c), lambda h, t: (h, t, 0, 0)),
        out_shape=jax.ShapeDtypeStruct((n_heads, n_tiles, r, c), jnp.float32),
        compiler_params=_params(("arbitrary", "arbitrary")),
        name="bias_tiles",
    )(rel_bias, jnp.asarray(bucket_np, jnp.int32))


def _mod_kernel(c_ref, w_ref, b_ref, o_ref):
    c = c_ref[...]
    c_act = (c * _sigmoid(c)).astype(MXU_DTYPE)
    o_ref[0] = _dot(c_act, w_ref[0].astype(MXU_DTYPE)) + b_ref[0]


def _mod_call(c_pad, w_mod, b_mod):
    depth, d, d6 = w_mod.shape
    tn = 1536
    return pl.pallas_call(
        _mod_kernel,
        grid=(depth, d6 // tn),
        in_specs=[
            pl.BlockSpec((SUBLANES, d), lambda l, j: (0, 0)),
            pl.BlockSpec((1, d, tn), lambda l, j: (l, 0, j)),
            pl.BlockSpec((1, 1, tn), lambda l, j: (l, 0, j)),
        ],
        out_specs=pl.BlockSpec((1, SUBLANES, tn), lambda l, j: (l, 0, j)),
        out_shape=jax.ShapeDtypeStruct((depth, SUBLANES, d6), jnp.float32),
        compiler_params=_params(("arbitrary", "arbitrary")),
        name="adaln_mod",
    )(c_pad, w_mod, b_mod.reshape(depth, 1, d6))


_C_QA = 0
_C_KVA = _C_QA + SWA_Q
_C_XR = _C_KVA + 2 * SWA_KV
_C_YR = _C_XR + RNN_WIDTH
_C_KC = _C_YR + RNN_WIDTH
_C_G = _C_KC + MOBA_W
_C_END = _C_G + 3 * D_MODEL


def _head_norm_rows(z, ones_blk, gain):
    zz = z * z
    hi = zz.astype(jnp.bfloat16)
    lo = (zz - hi.astype(jnp.float32)).astype(jnp.bfloat16)
    outs = []
    for c in range(z.shape[1] // LANES):
        sl = slice(c * LANES, (c + 1) * LANES)
        ss = _dot(hi[:, sl], ones_blk) + _dot(lo[:, sl], ones_blk)
        outs.append(z[:, sl] * lax.rsqrt(ss * (1.0 / HEAD_DIM) + RMS_EPS))
    y = outs[0] if len(outs) == 1 else jnp.concatenate(outs, axis=1)
    return y * gain


def _head_norm_cols(zt, gain_col):
    w, m = zt.shape
    z3 = zt.reshape(w // HEAD_DIM, HEAD_DIM, m)
    ss = jnp.sum(z3 * z3, axis=1, keepdims=True)
    y = z3 * lax.rsqrt(ss * (1.0 / HEAD_DIM) + RMS_EPS)
    return y.reshape(w, m) * gain_col


def _inproj_kernel(x_ref, mod_ref, gain_ref, wn_ref, wt_ref, ones_ref, gq_a_ref, gk_a_ref, gk_c_ref,
                   gq_c_ref, qa_ref, ka_ref, va_ref, xr_ref, yr_ref, kc_ref, kmean_ref, g_ref,
                   qct_ref, vct_ref):
    d = D_MODEL
    x = x_ref[...]
    shift = mod_ref[0, :, 0:d]
    scale = mod_ref[0, :, d:2 * d]
    ms = jnp.mean(x * x, axis=-1, keepdims=True)
    y = x * lax.rsqrt(ms + RMS_EPS) * gain_ref[...]
    h = (y * (1.0 + scale) + shift).astype(MXU_DTYPE)
    ones_blk = ones_ref[...]

    z = _dot(h, wn_ref[:, _C_QA:_C_KVA])
    qa_ref[...] = (_head_norm_rows(z, ones_blk, gq_a_ref[...]) * SM_SCALE).astype(qa_ref.dtype)

    z = _dot(h, wn_ref[:, _C_KVA:_C_XR])
    ka_ref[...] = _head_norm_rows(z[:, :SWA_KV], ones_blk, gk_a_ref[...]).astype(ka_ref.dtype)
    va_ref[...] = z[:, SWA_KV:].astype(va_ref.dtype)

    xr_ref[...] = _dot(h, wn_ref[:, _C_XR:_C_YR])
    yr_ref[...] = _dot(h, wn_ref[:, _C_YR:_C_KC])

    z = _dot(h, wn_ref[:, _C_KC:_C_G])
    kc = _head_norm_rows(z, ones_blk, gk_c_ref[...])
    kc_ref[...] = kc.astype(kc_ref.dtype)
    tm = x.shape[0]
    nblk = tm // MOBA_BLOCK
    kmean_ref[0] = jnp.mean(kc.reshape(nblk, MOBA_BLOCK, MOBA_W), axis=1)

    for c in range(3 * D_MODEL // 512):
        g_ref[:, c * 512:(c + 1) * 512] = _dot(h, wn_ref[:, _C_G + c * 512:_C_G + (c + 1) * 512])

    zt = _dot_nt(wt_ref[0:MOBA_W, :], h)
    qct = (_head_norm_cols(zt, gq_c_ref[...]) * SM_SCALE).astype(qct_ref.dtype)
    zt = _dot_nt(wt_ref[MOBA_W:2 * MOBA_W, :], h)
    vct = zt.astype(vct_ref.dtype)
    for r in range(nblk):
        qct_ref[0, r] = qct[:, r * MOBA_BLOCK:(r + 1) * MOBA_BLOCK]
        vct_ref[0, r] = vct[:, r * MOBA_BLOCK:(r + 1) * MOBA_BLOCK]


def _inproj_call(x2, mods_l, gain, wn, wt, ones_blk, gq_a, gk_a, gk_c, gq_c, b_sz, s_len):
    n, d = x2.shape
    tm = min(TM_IN, s_len)
    tpb = s_len // tm
    nblk = tm // MOBA_BLOCK
    nb = s_len // MOBA_BLOCK
    row = lambda i: (i, 0)
    const2 = lambda i: (0, 0)
    bf = MXU_DTYPE
    out_shape = (
        jax.ShapeDtypeStruct((n, SWA_Q), bf),
        jax.ShapeDtypeStruct((n, SWA_KV), bf),
        jax.ShapeDtypeStruct((n, SWA_KV), bf),
        jax.ShapeDtypeStruct((n, RNN_WIDTH), jnp.float32),
        jax.ShapeDtypeStruct((n, RNN_WIDTH), jnp.float32),
        jax.ShapeDtypeStruct((n, MOBA_W), bf),
        jax.ShapeDtypeStruct((n // tm, nblk, MOBA_W), jnp.float32),
        jax.ShapeDtypeStruct((n, 3 * D_MODEL), jnp.float32),
        jax.ShapeDtypeStruct((b_sz, nb, MOBA_W, MOBA_BLOCK), bf),
        jax.ShapeDtypeStruct((b_sz, nb, MOBA_W, MOBA_BLOCK), bf),
    )
    blk4 = pl.BlockSpec((1, nblk, MOBA_W, MOBA_BLOCK), lambda i: (i // tpb, i % tpb, 0, 0))
    out_specs = (
        pl.BlockSpec((tm, SWA_Q), row),
        pl.BlockSpec((tm, SWA_KV), row),
        pl.BlockSpec((tm, SWA_KV), row),
        pl.BlockSpec((tm, RNN_WIDTH), row),
        pl.BlockSpec((tm, RNN_WIDTH), row),
        pl.BlockSpec((tm, MOBA_W), row),
        pl.BlockSpec((1, nblk, MOBA_W), lambda i: (i, 0, 0)),
        pl.BlockSpec((tm, 3 * D_MODEL), row),
        blk4,
        blk4,
    )
    in_specs = [
        pl.BlockSpec((tm, d), row),
        pl.BlockSpec((1, 1, 6 * d), lambda i: (i // tpb, 0, 0)),
        pl.BlockSpec((1, d), const2),
        pl.BlockSpec(wn.shape, const2, pipeline_mode=pl.Buffered(1)),
        pl.BlockSpec(wt.shape, const2, pipeline_mode=pl.Buffered(1)),
        pl.BlockSpec(ones_blk.shape, const2),
        pl.BlockSpec(gq_a.shape, const2),
        pl.BlockSpec(gk_a.shape, const2),
        pl.BlockSpec(gk_c.shape, const2),
        pl.BlockSpec(gq_c.shape, const2),
    ]
    return pl.pallas_call(
        _inproj_kernel,
        grid=(n // tm,),
        in_specs=in_specs,
        out_specs=out_specs,
        out_shape=out_shape,
        compiler_params=_params(("arbitrary",)),
        name="inproj",
    )(x2, mods_l, gain, wn, wt, ones_blk, gq_a, gk_a, gk_c, gq_c)


def _swa_kernel(sink_ref, q_ref, kp_ref, ko_ref, vp_ref, vo_ref, bias_ref, o_ref, s_ref, p_ref):
    i = pl.program_id(1)
    L = SWA_BLOCK
    qi = lax.broadcasted_iota(jnp.int32, (L, 2 * L), 0)
    sj = lax.broadcasted_iota(jnp.int32, (L, 2 * L), 1)
    diff = qi + L - sj
    in_window = (diff >= 0) & (diff < SWA_WINDOW)
    lane = lax.broadcasted_iota(jnp.int32, (L, LANES), 1)
    low = lane < HEAD_DIM
    zero = jnp.zeros((), q_ref.dtype)
    for j in range(q_ref.shape[1] // L):
        rows = slice(j * L, (j + 1) * L)
        if j == 0:
            k_prev, v_prev = kp_ref[0], vp_ref[0]
            valid = in_window & ((i > 0) | (sj >= L))
        else:
            before = slice((j - 1) * L, j * L)
            k_prev, v_prev = ko_ref[0, before, :], vo_ref[0, before, :]
            valid = in_window
        kcat = jnp.concatenate([k_prev, ko_ref[0, rows, :]], axis=0)
        vcat = jnp.concatenate([v_prev, vo_ref[0, rows, :]], axis=0)
        for c in range(SWA_Q // LANES):
            q2 = q_ref[0, rows, c * LANES:(c + 1) * LANES]
            for g in range(SWA_KV_HEADS):
                head = g * SWA_GROUP + c
                qm = jnp.where(low if g == 0 else ~low, q2, zero)
                s_ref[j, head] = jnp.where(valid, _dot_nt(qm, kcat) + bias_ref[head], NEG_INF)
        for head in range(SWA_HEADS):
            s = s_ref[j, head]
            sink = sink_ref[head]
            m = jnp.maximum(jnp.max(s, axis=-1, keepdims=True), sink)
            p = jnp.exp(s - m)
            denom = jnp.sum(p, axis=-1, keepdims=True) + jnp.exp(sink - m)
            p_ref[j, head] = (p / denom).astype(p_ref.dtype)
        for c in range(SWA_Q // LANES):
            halves = [_dot(p_ref[j, g * SWA_GROUP + c], vcat) for g in range(SWA_KV_HEADS)]
            o_ref[0, rows, c * LANES:(c + 1) * LANES] = jnp.where(low, halves[0], halves[1]).astype(o_ref.dtype)


def _swa_call(qa, ka, va, sinks_l, bias_a, b_sz, s_len):
    L = SWA_BLOCK
    nb = min(SWA_STEP_BLOCKS, s_len // L)
    q3 = qa.reshape(b_sz, s_len, SWA_Q)
    k3 = ka.reshape(b_sz, s_len, SWA_KV)
    v3 = va.reshape(b_sz, s_len, SWA_KV)
    prev = lambda b, i: (b, jnp.maximum(i * nb - 1, 0), 0)
    own = lambda b, i: (b, i, 0)
    out = pl.pallas_call(
        _swa_kernel,
        grid=(b_sz, s_len // (nb * L)),
        in_specs=[
            pl.BlockSpec(memory_space=pltpu.SMEM),
            pl.BlockSpec((1, nb * L, SWA_Q), own),
            pl.BlockSpec((1, L, SWA_KV), prev),
            pl.BlockSpec((1, nb * L, SWA_KV), own),
            pl.BlockSpec((1, L, SWA_KV), prev),
            pl.BlockSpec((1, nb * L, SWA_KV), own),
            pl.BlockSpec((SWA_HEADS, L, 2 * L), lambda b, i: (0, 0, 0)),
        ],
        out_specs=pl.BlockSpec((1, nb * L, SWA_Q), own),
        out_shape=jax.ShapeDtypeStruct((b_sz, s_len, SWA_Q), MXU_DTYPE),
        scratch_shapes=[
            pltpu.VMEM((nb, SWA_HEADS, L, 2 * L), jnp.float32),
            pltpu.VMEM((nb, SWA_HEADS, L, 2 * L), MXU_DTYPE),
        ],
        compiler_params=_params(("arbitrary", "arbitrary")),
        name="swa",
    )(sinks_l, q3, k3, k3, v3, v3, bias_a)
    return out.reshape(b_sz * s_len, SWA_Q)


def _gelu_tanh(x):
    return 0.5 * x * (1.0 + jnp.tanh(math.sqrt(2.0 / math.pi) * (x + 0.044715 * (x * x * x))))


def _rnn_kernel(xr_ref, yr_ref, cw_ref, cb_ref, wg_ref, bg_ref, lam_ref, o_ref, xbuf_ref, h_ref):
    t = pl.program_id(2)
    T = xr_ref.shape[1]
    C = xr_ref.shape[2]
    H = SUBLANES

    @pl.when(t == 0)
    def _():
        xbuf_ref[...] = jnp.zeros(xbuf_ref.shape, jnp.float32)
        h_ref[...] = jnp.zeros(h_ref.shape, jnp.float32)

    x = xr_ref[0]
    halo = xbuf_ref[...]
    row8 = lax.broadcasted_iota(jnp.int32, (H, C), 0)
    xc = x * cw_ref[RNN_CONV - 1:RNN_CONV, :] + cb_ref[...]
    for j in range(1, RNN_CONV):
        xc = xc + _shift_rows(x, j, halo, row8) * cw_ref[RNN_CONV - 1 - j:RNN_CONV - j, :]
    xbuf_ref[...] = x[T - H:T, :]

    gates = _dot(xc.astype(MXU_DTYPE), wg_ref[0]) + bg_ref[0]
    r = _sigmoid(gates[:, :C])
    i = _sigmoid(gates[:, C:])
    nlam = -lam_ref[...]
    softplus = jnp.maximum(nlam, 0.0) + jnp.log1p(jnp.exp(-jnp.abs(nlam)))
    log_a = (-RGLRU_C) * r * softplus
    a = jnp.exp(log_a)
    mult = jnp.sqrt(jnp.tanh(-log_a) * (1.0 + a * a))
    row = lax.broadcasted_iota(jnp.int32, (T, C), 0)
    mult = jnp.where((row == 0) & (t == 0), 1.0, mult)
    u = mult * (i * xc)

    G = T // H
    a = a.reshape(G, H, C)
    u = u.reshape(G, H, C)
    sub = lax.broadcasted_iota(jnp.int32, (G, H, C), 1)
    step = 1
    while step < H:
        keep = sub >= step
        a_prev = jnp.where(keep, pltpu.roll(a, step, 1), 1.0)
        u_prev = jnp.where(keep, pltpu.roll(u, step, 1), 0.0)
        u = a * u_prev + u
        a = a * a_prev
        step *= 2
    carry = h_ref[0:1, :]
    groups = []
    for g in range(G):
        hg = u[g] + a[g] * carry
        groups.append(hg)
        carry = hg[H - 1:H, :]
    h_ref[0:1, :] = carry
    hs = jnp.concatenate(groups, axis=0)
    o_ref[0] = (hs * _gelu_tanh(yr_ref[0])).astype(o_ref.dtype)


def _rnn_call(xr, yr, conv_w, conv_b, wg, bg, lam, b_sz, s_len):
    T = min(T_RNN, s_len)
    C = C_RNN
    nc = RNN_WIDTH // C
    x3 = xr.reshape(b_sz, s_len, RNN_WIDTH)
    y3 = yr.reshape(b_sz, s_len, RNN_WIDTH)
    tile = lambda b, c, t: (b, t, c)
    chan = lambda b, c, t: (0, c)
    out = pl.pallas_call(
        _rnn_kernel,
        grid=(b_sz, nc, s_len // T),
        in_specs=[
            pl.BlockSpec((1, T, C), tile),
            pl.BlockSpec((1, T, C), tile),
            pl.BlockSpec((RNN_CONV, C), chan),
            pl.BlockSpec((1, C), chan),
            pl.BlockSpec((1, C, 2 * C), lambda b, c, t: (c, 0, 0)),
            pl.BlockSpec((1, 1, 2 * C), lambda b, c, t: (c, 0, 0)),
            pl.BlockSpec((1, C), chan),
        ],
        out_specs=pl.BlockSpec((1, T, C), tile),
        out_shape=jax.ShapeDtypeStruct((b_sz, s_len, RNN_WIDTH), MXU_DTYPE),
        scratch_shapes=[
            pltpu.VMEM((SUBLANES, C), jnp.float32),
            pltpu.VMEM((SUBLANES, C), jnp.float32),
        ],
        compiler_params=_params(("arbitrary", "arbitrary", "arbitrary")),
        name="rglru",
    )(x3, y3, conv_w, conv_b, wg, bg, lam)
    return out.reshape(b_sz * s_len, RNN_WIDTH)


def _split_heads(qt):
    rows = lax.broadcasted_iota(jnp.int32, qt.shape, 0)
    zero = jnp.zeros((), qt.dtype)
    return [jnp.where(rows < HEAD_DIM, qt, zero), jnp.where(rows >= HEAD_DIM, qt, zero)]


def _moba_kernel(far_ref, qt_ref, k_ref, vt_ref, kmean_ref, bias_ref, o_ref, sel_ref, qs_ref, s0_ref, s1_ref,
                 s2_ref, s3_ref, mx_ref, pva_ref, pvb_ref, ala_ref, alb_ref, m_ref, acc_ref):
    n = pl.program_id(2)
    MB = MOBA_BLOCK
    nb = kmean_ref.shape[1]
    qts = _split_heads(qt_ref[0, 0])

    kmean = kmean_ref[0].astype(MXU_DTYPE)
    blk = lax.broadcasted_iota(jnp.int32, (nb, MB), 0).astype(jnp.float32)
    n_f = n.astype(jnp.float32)
    past = blk < n_f
    for e in range(2):
        gate = _dot(kmean, qts[e])
        gate = jnp.where(past, gate, NEG_INF)
        sel = jnp.zeros((nb, MB), jnp.float32)
        for _ in range(MOBA_TOPK):
            mx = jnp.max(gate, axis=0, keepdims=True)
            first = jnp.min(jnp.where(gate == mx, blk, float(nb)), axis=0, keepdims=True)
            hit = blk == first
            sel = jnp.where(hit, 1.0, sel)
            gate = jnp.where(hit, TAKEN, gate)
        sel_ref[e] = jnp.where(blk == n_f, 1.0, jnp.where(past, sel, 0.0))

    for e in range(2):
        qs_ref[e] = qts[e]
    m_ref[...] = jnp.full(m_ref.shape, NEG_INF, jnp.float32)
    acc_ref[...] = jnp.zeros(acc_ref.shape, jnp.float32)
    pvb_ref[...] = jnp.zeros(pvb_ref.shape, jnp.float32)
    alb_ref[...] = jnp.ones(alb_ref.shape, jnp.float32)
    ones_rows = jnp.ones((MOBA_SUM_ROWS, MB), MXU_DTYPE)

    def block_of(t):
        return jnp.clip(n - t, 0, n)

    def accumulate(pv_ref, al_ref):
        for e in range(2):
            acc_ref[e] = al_ref[e] * acc_ref[e] + pv_ref[e]

    far_bias = [far_ref[2 * pl.program_id(1) + e] for e in range(2)]

    s_bufs = (s0_ref, s1_ref, s2_ref, s3_ref)

    def scores(t, slot, far):
        m = block_of(t)
        k_m = k_ref[0, pl.ds(pl.multiple_of(m * MB, MB), MB), :]
        for e in range(2):
            dims = slice(e * HEAD_DIM, (e + 1) * HEAD_DIM)
            s = _dot(k_m[:, dims], qs_ref[e, dims, :])
            if not far:
                s = s + bias_ref[e, t]
            s_bufs[slot][e] = s
            part = s[0:SUBLANES, :]
            for g in range(1, MB // SUBLANES):
                part = jnp.maximum(part, s[g * SUBLANES:(g + 1) * SUBLANES, :])
            mx_ref[slot, e] = part

    def softmax(t, slot, pv_ref, al_ref, far):
        m = block_of(t)
        live = t <= n
        vt_m = vt_ref[0, m]
        for e in range(2):
            s = s_bufs[slot][e]
            chosen = (sel_ref[e, pl.ds(m, 1), :] > 0.0) & live
            mx = jnp.max(mx_ref[slot, e], axis=0, keepdims=True)
            if far:
                mx = mx + far_bias[e]
            mx = jnp.where(chosen, mx, NEG_INF)
            m_old = m_ref[e]
            m_new = jnp.maximum(m_old, mx)
            al_ref[e] = jnp.exp(m_old - m_new)
            sub = m_new - far_bias[e] if far else m_new
            p = jnp.exp(s - jnp.where(chosen, sub, -NEG_INF)).astype(MXU_DTYPE)
            m_ref[e] = m_new
            lhs = jnp.concatenate([vt_m[e * HEAD_DIM:(e + 1) * HEAD_DIM, :], ones_rows], axis=0)
            pv_ref[e] = _dot(lhs, p)

    pv_bufs = ((pva_ref, ala_ref), (pvb_ref, alb_ref))

    def region(t, slot, far, far_ahead):
        pv_cur, al_cur = pv_bufs[slot % 2]
        pv_oth, al_oth = pv_bufs[1 - slot % 2]
        scores(t + MOBA_AHEAD, (slot + MOBA_AHEAD) % MOBA_UNROLL, far_ahead)
        accumulate(pv_oth, al_oth)
        softmax(t, slot, pv_cur, al_cur, far)

    def near_regions(t_lo, t_hi):
        for t in range(t_lo, t_hi):
            region(t, t % MOBA_UNROLL, t >= MOBA_NEAR, t + MOBA_AHEAD >= MOBA_NEAR)

    near_steps = 2 * MOBA_UNROLL
    for t in range(MOBA_AHEAD):
        scores(t, t, False)
    near_regions(0, MOBA_UNROLL)

    @pl.when(n >= MOBA_UNROLL)
    def _():
        near_regions(MOBA_UNROLL, near_steps)

    def far_regions(t0, slot0, count):
        for r in range(count):
            region(t0 + r, slot0 + r, True, True)

    n_far = jnp.maximum(n + 1 - near_steps, 0)
    n_quads = (n_far + 1) // MOBA_UNROLL
    n_pairs = jnp.where(n_far > MOBA_UNROLL * n_quads, 1, 0)

    def quad_body(i, carry):
        far_regions(near_steps + MOBA_UNROLL * i, 0, MOBA_UNROLL)
        return carry

    lax.fori_loop(0, n_quads, quad_body, 0)
    t_pairs = near_steps + MOBA_UNROLL * n_quads

    @pl.when(n_pairs > 0)
    def _():
        far_regions(t_pairs, 0, 2)

    accumulate(pvb_ref, alb_ref)

    top = acc_ref[0, 0:HEAD_DIM, :] / acc_ref[0, HEAD_DIM:HEAD_DIM + 1, :]
    bot = acc_ref[1, 0:HEAD_DIM, :] / acc_ref[1, HEAD_DIM:HEAD_DIM + 1, :]
    o_ref[0] = jnp.concatenate([top, bot], axis=0).T.astype(o_ref.dtype)


def _moba_call(qct, kc, vct, kmean, bias_c, far_c, b_sz, s_len):
    MB = MOBA_BLOCK
    nb = s_len // MB
    k3 = kc.reshape(b_sz, s_len, MOBA_W)
    km3 = kmean.reshape(b_sz, nb, MOBA_W)
    npair = MOBA_W // LANES
    out = pl.pallas_call(
        _moba_kernel,
        grid=(b_sz, npair, nb),
        in_specs=[
            pl.BlockSpec(memory_space=pltpu.SMEM),
            pl.BlockSpec((1, 1, LANES, MB), lambda b, p, n: (b, n, p, 0)),
            pl.BlockSpec((1, s_len, LANES), lambda b, p, n: (b, 0, p)),
            pl.BlockSpec((1, nb, LANES, MB), lambda b, p, n: (b, 0, p, 0)),
            pl.BlockSpec((1, nb, LANES), lambda b, p, n: (b, 0, p)),
            pl.BlockSpec((2, MOBA_NEAR, MB, MB), lambda b, p, n: (p, 0, 0, 0)),
        ],
        out_specs=pl.BlockSpec((1, MB, LANES), lambda b, p, n: (b, n, p)),
        out_shape=jax.ShapeDtypeStruct((b_sz, s_len, MOBA_W), MXU_DTYPE),
        scratch_shapes=[
            pltpu.VMEM((2, nb, MB), jnp.float32),
            pltpu.VMEM((2, LANES, MB), MXU_DTYPE),
            pltpu.VMEM((2, MB, MB), jnp.float32),
            pltpu.VMEM((2, MB, MB), jnp.float32),
            pltpu.VMEM((2, MB, MB), jnp.float32),
            pltpu.VMEM((2, MB, MB), jnp.float32),
            pltpu.VMEM((MOBA_UNROLL, 2, SUBLANES, MB), jnp.float32),
            pltpu.VMEM((2, HEAD_DIM + MOBA_SUM_ROWS, MB), jnp.float32),
            pltpu.VMEM((2, HEAD_DIM + MOBA_SUM_ROWS, MB), jnp.float32),
            pltpu.VMEM((2, 1, MB), jnp.float32),
            pltpu.VMEM((2, 1, MB), jnp.float32),
            pltpu.VMEM((2, 1, MB), jnp.float32),
            pltpu.VMEM((2, HEAD_DIM + MOBA_SUM_ROWS, MB), jnp.float32),
        ],
        compiler_params=_params(("arbitrary", "arbitrary", "arbitrary")),
        name="moba",
    )(far_c, qct, k3, vct, km3, bias_c)
    return out.reshape(b_sz * s_len, MOBA_W)


def _merge_kernel(x_ref, oa_ref, ob_ref, oc_ref, g_ref, mod_ref, wa_ref, wb_ref, wc_ref, wo_ref, o_ref):
    d = D_MODEL
    merged = _sigmoid(g_ref[:, 0:d]) * _dot(oa_ref[...], wa_ref[...])
    merged = merged + _sigmoid(g_ref[:, d:2 * d]) * _dot(ob_ref[...], wb_ref[...])
    merged = merged + _sigmoid(g_ref[:, 2 * d:3 * d]) * _dot(oc_ref[...], wc_ref[...])
    y = _dot(merged.astype(MXU_DTYPE), wo_ref[...])
    gate_m = mod_ref[0, :, 2 * d:3 * d]
    o_ref[...] = x_ref[...] + gate_m * y


def _merge_call(x2, o_a, o_b, o_c, g_logits, mods_l, wa, wb, wc, wo, s_len):
    n, d = x2.shape
    tm = min(TM_MERGE, s_len)
    tpb = s_len // tm
    row = lambda i: (i, 0)
    const2 = lambda i: (0, 0)
    return pl.pallas_call(
        _merge_kernel,
        grid=(n // tm,),
        in_specs=[
            pl.BlockSpec((tm, d), row),
            pl.BlockSpec((tm, SWA_Q), row),
            pl.BlockSpec((tm, RNN_WIDTH), row),
            pl.BlockSpec((tm, MOBA_W), row),
            pl.BlockSpec((tm, 3 * d), row),
            pl.BlockSpec((1, 1, 6 * d), lambda i: (i // tpb, 0, 0)),
            pl.BlockSpec(wa.shape, const2),
            pl.BlockSpec(wb.shape, const2),
            pl.BlockSpec(wc.shape, const2),
            pl.BlockSpec(wo.shape, const2),
        ],
        out_specs=pl.BlockSpec((tm, d), row),
        out_shape=jax.ShapeDtypeStruct((n, d), jnp.float32),
        compiler_params=_params(("arbitrary",)),
        name="merge",
    )(x2, o_a, o_b, o_c, g_logits, mods_l, wa, wb, wc, wo)


def _ffn_kernel(x_ref, mod_ref, gain_ref, wu_ref, cw_ref, cb_ref, wd_ref, o_ref, h_ref, act_ref, halo_ref,
                *, tiles_per_seq):
    i = pl.program_id(0)
    d = D_MODEL
    tm = x_ref.shape[0]
    H = SUBLANES
    CH = FFN_CHUNK

    @pl.when(i == 0)
    def _():
        halo_ref[...] = jnp.zeros(halo_ref.shape, jnp.float32)

    x = x_ref[...]
    shift = mod_ref[0, :, 3 * d:4 * d]
    scale = mod_ref[0, :, 4 * d:5 * d]
    ms = jnp.mean(x * x, axis=-1, keepdims=True)
    y = x * lax.rsqrt(ms + RMS_EPS) * gain_ref[...]
    h_ref[...] = (y * (1.0 + scale) + shift).astype(h_ref.dtype)

    seq_start = (i % tiles_per_seq) == 0
    row8 = lax.broadcasted_iota(jnp.int32, (H, CH), 0)

    def conv(col):
        cols = slice(col, col + CH)
        u = _dot(h_ref[...], wu_ref[:, cols])
        halo = jnp.where(seq_start, 0.0, halo_ref[:, cols])
        halo_ref[:, cols] = u[tm - H:tm, :]
        out = u * cw_ref[FFN_CONV - 1:FFN_CONV, cols] + cb_ref[:, cols]
        for k in range(1, FFN_CONV):
            out = out + _shift_rows(u, k, halo, row8) * cw_ref[FFN_CONV - 1 - k:FFN_CONV - k, cols]
        return out

    for c in range(D_FF // CH):
        ug = conv(c * CH)
        uv = conv(D_FF + c * CH)
        act_ref[:, c * CH:(c + 1) * CH] = (ug * _sigmoid(ug) * uv).astype(act_ref.dtype)

    gate_f = mod_ref[0, :, 5 * d:6 * d]
    o_ref[...] = x_ref[...] + gate_f * _dot(act_ref[...], wd_ref[...])


def _ffn_call(x2, mods_l, gain, w_up, conv_w, conv_b, w_down, s_len):
    n, d = x2.shape
    tm = min(TM_FFN, s_len)
    tpb = s_len // tm
    row = lambda i: (i, 0)
    const2 = lambda i: (0, 0)
    kern = functools.partial(_ffn_kernel, tiles_per_seq=tpb)
    return pl.pallas_call(
        kern,
        grid=(n // tm,),
        in_specs=[
            pl.BlockSpec((tm, d), row),
            pl.BlockSpec((1, 1, 6 * d), lambda i: (i // tpb, 0, 0)),
            pl.BlockSpec((1, d), const2),
            pl.BlockSpec(w_up.shape, const2, pipeline_mode=pl.Buffered(1)),
            pl.BlockSpec(conv_w.shape, const2),
            pl.BlockSpec(conv_b.shape, const2),
            pl.BlockSpec(w_down.shape, const2, pipeline_mode=pl.Buffered(1)),
        ],
        out_specs=pl.BlockSpec((tm, d), row),
        out_shape=jax.ShapeDtypeStruct((n, d), jnp.float32),
        scratch_shapes=[
            pltpu.VMEM((tm, d), MXU_DTYPE),
            pltpu.VMEM((tm, D_FF), MXU_DTYPE),
            pltpu.VMEM((SUBLANES, 2 * D_FF), jnp.float32),
        ],
        compiler_params=_params(("arbitrary",)),
        name="ffn",
    )(x2, mods_l, gain, w_up, conv_w, conv_b, w_down)


def _swa_head_order():
    order = []
    for c in range(SWA_Q // LANES):
        for g in range(SWA_KV_HEADS):
            order.append(g * SWA_GROUP + c)
    return order


def _bias_tables(rel_bias):
    L = SWA_BLOCK
    qi = np.arange(L)[:, None]
    sj = np.arange(2 * L)[None, :]
    bucket_a = _t5_bucket_np(qi + L - sj)[None]
    tab_a = _bias_tiles_call(rel_bias, bucket_a, 0, SWA_HEADS)[:, 0]
    MB = MOBA_BLOCK
    kj = np.arange(MB)[:, None]
    qq = np.arange(MB)[None, :]
    deltas = np.arange(MOBA_NEAR)[:, None, None]
    bucket_c = _t5_bucket_np(deltas * MB + qq[None] - kj[None])
    bucket_c = np.where((deltas > 0) | (kj <= qq)[None], bucket_c, NUM_BUCKETS)
    tab_c = _bias_tiles_call(rel_bias, bucket_c, SWA_HEADS, MOBA_HEADS)
    far_c = rel_bias[NUM_BUCKETS - 1, SWA_HEADS:]
    return tab_a, tab_c, far_c


def kernel(x, c, w_mod, b_mod, norm_mix, norm_ffn, w_in, qnorm_a, knorm_a, sinks, rnn_conv_w, rnn_conv_b,
           rnn_gate_a_w, rnn_gate_a_b, rnn_gate_x_w, rnn_gate_x_b, rnn_lambda, qnorm_c, knorm_c, rel_bias,
           w_branch, w_out, w_up, ffn_conv_w, ffn_conv_b, w_down):
    b_sz, s_len, d = x.shape
    depth = w_in.shape[0]
    assert d == D_MODEL and s_len % MOBA_BLOCK == 0 and b_sz <= SUBLANES
    n = b_sz * s_len
    bf = MXU_DTYPE
    f32 = jnp.float32

    c_pad = jnp.zeros((SUBLANES, d), f32).at[:b_sz].set(c)
    mods = _mod_call(c_pad, w_mod, b_mod)[:, :b_sz]
    mods = mods.reshape(depth, b_sz, 1, 6 * d)

    offs = np.cumsum([0, SWA_Q, SWA_KV, SWA_KV, RNN_WIDTH, RNN_WIDTH, MOBA_W, MOBA_W, MOBA_W, 3 * d])
    o_qa, o_ka, o_va, o_xr, o_yr, o_qc, o_kc, o_vc, o_g, o_end = [int(v) for v in offs]
    head_order = _swa_head_order()
    w_in_b = w_in.astype(bf)
    wn_all = jnp.concatenate(
        [w_in_b[:, :, o_qa + h * HEAD_DIM:o_qa + (h + 1) * HEAD_DIM] for h in head_order]
        + [w_in_b[:, :, o_ka:o_qc], w_in_b[:, :, o_kc:o_vc], w_in_b[:, :, o_g:o_end]], axis=-1)
    wt_all = jnp.swapaxes(jnp.concatenate([w_in_b[:, :, o_qc:o_kc], w_in_b[:, :, o_vc:o_g]], axis=-1), 1, 2)
    lane_head = np.arange(LANES) // HEAD_DIM
    ones_blk = jnp.asarray(lane_head[:, None] == lane_head[None, :], bf)

    wa_all = jnp.concatenate(
        [w_branch[:, h * HEAD_DIM:(h + 1) * HEAD_DIM] for h in head_order], axis=1).astype(bf)
    wb_all = w_branch[:, SWA_Q:SWA_Q + RNN_WIDTH].astype(bf)
    wc_all = w_branch[:, SWA_Q + RNN_WIDTH:].astype(bf)
    wo_all = w_out.astype(bf)
    wup_all = w_up.astype(bf)
    wdn_all = w_down.astype(bf)

    per = C_RNN // RNN_BLOCK_WIDTH
    nct = RNN_WIDTH // C_RNN
    eye = jnp.eye(per, dtype=f32)

    def blockdiag(w):
        w5 = w.reshape(depth, nct, per, RNN_BLOCK_WIDTH, RNN_BLOCK_WIDTH)
        full = jnp.einsum('lcpij,pq->lcpiqj', w5, eye)
        return full.reshape(depth, nct, C_RNN, C_RNN)

    wg_all = jnp.concatenate([blockdiag(rnn_gate_a_w), blockdiag(rnn_gate_x_w)], axis=-1).astype(bf)
    bg_all = jnp.concatenate([rnn_gate_a_b.reshape(depth, nct, 1, C_RNN),
                              rnn_gate_x_b.reshape(depth, nct, 1, C_RNN)], axis=-1)

    tab_a, tab_c, far_c = _bias_tables(rel_bias)

    x2 = x.reshape(n, d)
    for l in range(depth):
        gq_a = jnp.tile(qnorm_a[l], SWA_Q // HEAD_DIM)[None, :]
        gk_a = jnp.tile(knorm_a[l], SWA_KV // HEAD_DIM)[None, :]
        gk_c = jnp.tile(knorm_c[l], MOBA_W // HEAD_DIM)[None, :]
        gq_c = jnp.tile(qnorm_c[l], MOBA_W // HEAD_DIM)[:, None]
        qa, ka, va, xr, yr, kc, kmean, g_logits, qct, vct = _inproj_call(
            x2, mods[l], norm_mix[l][None, :], wn_all[l], wt_all[l], ones_blk, gq_a, gk_a, gk_c, gq_c,
            b_sz, s_len)
        o_a = _swa_call(qa, ka, va, sinks[l], tab_a, b_sz, s_len)
        o_b = _rnn_call(xr, yr, rnn_conv_w[l], rnn_conv_b[l][None, :], wg_all[l], bg_all[l],
                        rnn_lambda[l][None, :], b_sz, s_len)
        o_c = _moba_call(qct, kc, vct, kmean, tab_c, far_c, b_sz, s_len)
        x2 = _merge_call(x2, o_a, o_b, o_c, g_logits, mods[l], wa_all[l], wb_all[l], wc_all[l], wo_all[l],
                         s_len)
        x2 = _ffn_call(x2, mods[l], norm_ffn[l][None, :], wup_all[l], ffn_conv_w[l], ffn_conv_b[l][None, :],
                       wdn_all[l], s_len)
    return x2.reshape(b_sz, s_len, d)
```

```python
import functools
import math

import numpy as np
import jax
import jax.numpy as jnp
from jax import lax
from jax.experimental import pallas as pl
from jax.experimental.pallas import tpu as pltpu

D_MODEL = 1024
DEPTH = 4
HEAD_DIM = 64
SWA_HEADS = 8
SWA_KV_HEADS = 2
SWA_WINDOW = 128
SWA_BLOCK = 128
RNN_WIDTH = D_MODEL
RNN_BLOCKS = 16
RNN_BLOCK_WIDTH = RNN_WIDTH // RNN_BLOCKS
RNN_CONV = 4
RGLRU_C = 8.0
MOBA_HEADS = 8
MOBA_BLOCK = 256
MOBA_TOPK = 3
NUM_BUCKETS = 32
MAX_DISTANCE = 2048
D_FF = 2816
FFN_CONV = 3
RMS_EPS = 1e-6
NEG_INF = -1e30
TAKEN = -3e38
SWA_Q = SWA_HEADS * HEAD_DIM
SWA_KV = SWA_KV_HEADS * HEAD_DIM
MOBA_W = MOBA_HEADS * HEAD_DIM
SWA_GROUP = SWA_HEADS // SWA_KV_HEADS

LANES = 128
SUBLANES = 8
VMEM_LIMIT_BYTES = 56 * 1024 * 1024

MXU_DTYPE = jnp.bfloat16
SM_SCALE = HEAD_DIM ** -0.5

TM_IN = 512
TM_MERGE = 512
TM_FFN = 512
FFN_CHUNK = 256
T_RNN = 1024
SWA_STEP_BLOCKS = 4
C_RNN = 256


def _params(semantics):
    return pltpu.CompilerParams(dimension_semantics=semantics, vmem_limit_bytes=VMEM_LIMIT_BYTES)


def _dot(a, b):
    return jnp.dot(a, b, preferred_element_type=jnp.float32)


def _dot_nt(a, b):
    return lax.dot_general(a, b, (((1,), (1,)), ((), ())), preferred_element_type=jnp.float32)


def _sigmoid(x):
    return 1.0 / (1.0 + jnp.exp(-x))


def _shift_rows(u, k, halo, row8):
    H = SUBLANES
    r = pltpu.roll(u, k, 0)
    top = jnp.where(row8 < k, pltpu.roll(halo, k, 0), r[0:H, :])
    return jnp.concatenate([top, r[H:, :]], axis=0)


def _t5_bucket_np(dist):
    dist = np.maximum(dist, 0)
    max_exact = NUM_BUCKETS // 2
    ratio = np.maximum(dist, 1).astype(np.float32) / np.float32(max_exact)
    log_ratio = np.log(ratio).astype(np.float32) / np.float32(math.log(MAX_DISTANCE / max_exact))
    large = max_exact + (log_ratio * np.float32(NUM_BUCKETS - max_exact)).astype(np.int32)
    large = np.minimum(large, NUM_BUCKETS - 1)
    return np.where(dist < max_exact, dist, large).astype(np.int32)


def _moba_near_blocks():
    delta = 0
    while True:
        lo = max(delta * MOBA_BLOCK - (MOBA_BLOCK - 1), 0)
        if int(_t5_bucket_np(np.array([lo]))[0]) == NUM_BUCKETS - 1:
            return delta
        delta += 1


MOBA_NEAR = _moba_near_blocks()
MOBA_SUM_ROWS = 2 * SUBLANES
MOBA_UNROLL = 4
MOBA_AHEAD = 1
assert MOBA_NEAR <= 2 * MOBA_UNROLL and MOBA_UNROLL == 4 and MOBA_AHEAD < MOBA_UNROLL


def _bias_tile_kernel(tab_ref, bucket_ref, o_ref, *, head0):
    h = pl.program_id(0) + head0
    bucket = bucket_ref[0]
    out = jnp.full(bucket.shape, NEG_INF, jnp.float32)
    for b in range(NUM_BUCKETS):
        out = jnp.where(bucket == b, tab_ref[b, h], out)
    o_ref[0, 0] = out


def _bias_tiles_call(rel_bias, bucket_np, head0, n_heads):
    n_tiles, r, c = bucket_np.shape
    return pl.pallas_call(
        functools.partial(_bias_tile_kernel, head0=head0),
        grid=(n_heads, n_tiles),
        in_specs=[
            pl.BlockSpec(memory_space=pltpu.SMEM),
            pl.BlockSpec((1, r, c), lambda h, t: (t, 0, 0)),
        ],
        out_specs=pl.BlockSpec((1, 1, r, c), lambda h, t: (h, t, 0, 0)),
        out_shape=jax.ShapeDtypeStruct((n_heads, n_tiles, r, c), jnp.float32),
        compiler_params=_params(("arbitrary", "arbitrary")),
        name="bias_tiles",
    )(rel_bias, jnp.asarray(bucket_np, jnp.int32))


def _mod_kernel(c_ref, w_ref, b_ref, o_ref):
    c = c_ref[...]
    c_act = (c * _sigmoid(c)).astype(MXU_DTYPE)
    o_ref[0] = _dot(c_act, w_ref[0].astype(MXU_DTYPE)) + b_ref[0]


def _mod_call(c_pad, w_mod, b_mod):
    depth, d, d6 = w_mod.shape
    tn = 1536
    return pl.pallas_call(
        _mod_kernel,
        grid=(depth, d6 // tn),
        in_specs=[
            pl.BlockSpec((SUBLANES, d), lambda l, j: (0, 0)),
            pl.BlockSpec((1, d, tn), lambda l, j: (l, 0, j)),
            pl.BlockSpec((1, 1, tn), lambda l, j: (l, 0, j)),
        ],
        out_specs=pl.BlockSpec((1, SUBLANES, tn), lambda l, j: (l, 0, j)),
        out_shape=jax.ShapeDtypeStruct((depth, SUBLANES, d6), jnp.float32),
        compiler_params=_params(("arbitrary", "arbitrary")),
        name="adaln_mod",
    )(c_pad, w_mod, b_mod.reshape(depth, 1, d6))


_C_QA = 0
_C_KVA = _C_QA + SWA_Q
_C_XR = _C_KVA + 2 * SWA_KV
_C_YR = _C_XR + RNN_WIDTH
_C_KC = _C_YR + RNN_WIDTH
_C_G = _C_KC + MOBA_W
_C_END = _C_G + 3 * D_MODEL


def _head_norm_rows(z, ones_blk, gain):
    zz = z * z
    hi = zz.astype(jnp.bfloat16)
    lo = (zz - hi.astype(jnp.float32)).astype(jnp.bfloat16)
    outs = []
    for c in range(z.shape[1] // LANES):
        sl = slice(c * LANES, (c + 1) * LANES)
        ss = _dot(hi[:, sl], ones_blk) + _dot(lo[:, sl], ones_blk)
        outs.append(z[:, sl] * lax.rsqrt(ss * (1.0 / HEAD_DIM) + RMS_EPS))
    y = outs[0] if len(outs) == 1 else jnp.concatenate(outs, axis=1)
    return y * gain


def _head_norm_cols(zt, gain_col):
    w, m = zt.shape
    z3 = zt.reshape(w // HEAD_DIM, HEAD_DIM, m)
    ss = jnp.sum(z3 * z3, axis=1, keepdims=True)
    y = z3 * lax.rsqrt(ss * (1.0 / HEAD_DIM) + RMS_EPS)
    return y.reshape(w, m) * gain_col


def _inproj_kernel(x_ref, mod_ref, gain_ref, wn_ref, wt_ref, ones_ref, gq_a_ref, gk_a_ref, gk_c_ref,
                   gq_c_ref, qa_ref, ka_ref, va_ref, xr_ref, yr_ref, kc_ref, kmean_ref, g_ref,
                   qct_ref, vct_ref):
    d = D_MODEL
    x = x_ref[...]
    shift = mod_ref[0, :, 0:d]
    scale = mod_ref[0, :, d:2 * d]
    ms = jnp.mean(x * x, axis=-1, keepdims=True)
    y = x * lax.rsqrt(ms + RMS_EPS) * gain_ref[...]
    h = (y * (1.0 + scale) + shift).astype(MXU_DTYPE)
    ones_blk = ones_ref[...]

    z = _dot(h, wn_ref[:, _C_QA:_C_KVA])
    qa_ref[...] = (_head_norm_rows(z, ones_blk, gq_a_ref[...]) * SM_SCALE).astype(qa_ref.dtype)

    z = _dot(h, wn_ref[:, _C_KVA:_C_XR])
    ka_ref[...] = _head_norm_rows(z[:, :SWA_KV], ones_blk, gk_a_ref[...]).astype(ka_ref.dtype)
    va_ref[...] = z[:, SWA_KV:].astype(va_ref.dtype)

    xr_ref[...] = _dot(h, wn_ref[:, _C_XR:_C_YR])
    yr_ref[...] = _dot(h, wn_ref[:, _C_YR:_C_KC])

    z = _dot(h, wn_ref[:, _C_KC:_C_G])
    kc = _head_norm_rows(z, ones_blk, gk_c_ref[...])
    kc_ref[...] = kc.astype(kc_ref.dtype)
    tm = x.shape[0]
    nblk = tm // MOBA_BLOCK
    kmean_ref[0] = jnp.mean(kc.reshape(nblk, MOBA_BLOCK, MOBA_W), axis=1)

    for c in range(3 * D_MODEL // 512):
        g_ref[:, c * 512:(c + 1) * 512] = _dot(h, wn_ref[:, _C_G + c * 512:_C_G + (c + 1) * 512])

    zt = _dot_nt(wt_ref[0:MOBA_W, :], h)
    qct = (_head_norm_cols(zt, gq_c_ref[...]) * SM_SCALE).astype(qct_ref.dtype)
    zt = _dot_nt(wt_ref[MOBA_W:2 * MOBA_W, :], h)
    vct = zt.astype(vct_ref.dtype)
    for r in range(nblk):
        qct_ref[0, r] = qct[:, r * MOBA_BLOCK:(r + 1) * MOBA_BLOCK]
        vct_ref[0, r] = vct[:, r * MOBA_BLOCK:(r + 1) * MOBA_BLOCK]


def _inproj_call(x2, mods_l, gain, wn, wt, ones_blk, gq_a, gk_a, gk_c, gq_c, b_sz, s_len):
    n, d = x2.shape
    tm = min(TM_IN, s_len)
    tpb = s_len // tm
    nblk = tm // MOBA_BLOCK
    nb = s_len // MOBA_BLOCK
    row = lambda i: (i, 0)
    const2 = lambda i: (0, 0)
    bf = MXU_DTYPE
    out_shape = (
        jax.ShapeDtypeStruct((n, SWA_Q), bf),
        jax.ShapeDtypeStruct((n, SWA_KV), bf),
        jax.ShapeDtypeStruct((n, SWA_KV), bf),
        jax.ShapeDtypeStruct((n, RNN_WIDTH), jnp.float32),
        jax.ShapeDtypeStruct((n, RNN_WIDTH), jnp.float32),
        jax.ShapeDtypeStruct((n, MOBA_W), bf),
        jax.ShapeDtypeStruct((n // tm, nblk, MOBA_W), jnp.float32),
        jax.ShapeDtypeStruct((n, 3 * D_MODEL), jnp.float32),
        jax.ShapeDtypeStruct((b_sz, nb, MOBA_W, MOBA_BLOCK), bf),
        jax.ShapeDtypeStruct((b_sz, nb, MOBA_W, MOBA_BLOCK), bf),
    )
    blk4 = pl.BlockSpec((1, nblk, MOBA_W, MOBA_BLOCK), lambda i: (i // tpb, i % tpb, 0, 0))
    out_specs = (
        pl.BlockSpec((tm, SWA_Q), row),
        pl.BlockSpec((tm, SWA_KV), row),
        pl.BlockSpec((tm, SWA_KV), row),
        pl.BlockSpec((tm, RNN_WIDTH), row),
        pl.BlockSpec((tm, RNN_WIDTH), row),
        pl.BlockSpec((tm, MOBA_W), row),
        pl.BlockSpec((1, nblk, MOBA_W), lambda i: (i, 0, 0)),
        pl.BlockSpec((tm, 3 * D_MODEL), row),
        blk4,
        blk4,
    )
    in_specs = [
        pl.BlockSpec((tm, d), row),
        pl.BlockSpec((1, 1, 6 * d), lambda i: (i // tpb, 0, 0)),
        pl.BlockSpec((1, d), const2),
        pl.BlockSpec(wn.shape, const2, pipeline_mode=pl.Buffered(1)),
        pl.BlockSpec(wt.shape, const2, pipeline_mode=pl.Buffered(1)),
        pl.BlockSpec(ones_blk.shape, const2),
        pl.BlockSpec(gq_a.shape, const2),
        pl.BlockSpec(gk_a.shape, const2),
        pl.BlockSpec(gk_c.shape, const2),
        pl.BlockSpec(gq_c.shape, const2),
    ]
    return pl.pallas_call(
        _inproj_kernel,
        grid=(n // tm,),
        in_specs=in_specs,
        out_specs=out_specs,
        out_shape=out_shape,
        compiler_params=_params(("arbitrary",)),
        name="inproj",
    )(x2, mods_l, gain, wn, wt, ones_blk, gq_a, gk_a, gk_c, gq_c)


def _swa_kernel(sink_ref, q_ref, kp_ref, ko_ref, vp_ref, vo_ref, bias_ref, o_ref, s_ref, p_ref):
    i = pl.program_id(1)
    L = SWA_BLOCK
    qi = lax.broadcasted_iota(jnp.int32, (L, 2 * L), 0)
    sj = lax.broadcasted_iota(jnp.int32, (L, 2 * L), 1)
    diff = qi + L - sj
    in_window = (diff >= 0) & (diff < SWA_WINDOW)
    lane = lax.broadcasted_iota(jnp.int32, (L, LANES), 1)
    low = lane < HEAD_DIM
    zero = jnp.zeros((), q_ref.dtype)
    for j in range(q_ref.shape[1] // L):
        rows = slice(j * L, (j + 1) * L)
        if j == 0:
            k_prev, v_prev = kp_ref[0], vp_ref[0]
            valid = in_window & ((i > 0) | (sj >= L))
        else:
            before = slice((j - 1) * L, j * L)
            k_prev, v_prev = ko_ref[0, before, :], vo_ref[0, before, :]
            valid = in_window
        kcat = jnp.concatenate([k_prev, ko_ref[0, rows, :]], axis=0)
        vcat = jnp.concatenate([v_prev, vo_ref[0, rows, :]], axis=0)
        k_heads = [kcat[:, g * HEAD_DIM:(g + 1) * HEAD_DIM] for g in range(SWA_KV_HEADS)]
        for c in range(SWA_Q // LANES):
            q2 = q_ref[0, rows, c * LANES:(c + 1) * LANES]
            for g in range(SWA_KV_HEADS):
                head = g * SWA_GROUP + c
                q_head = q2[:, g * HEAD_DIM:(g + 1) * HEAD_DIM]
                s_ref[j, head] = jnp.where(valid, _dot_nt(q_head, k_heads[g]) + bias_ref[head], NEG_INF)
        for head in range(SWA_HEADS):
            s = s_ref[j, head]
            sink = sink_ref[head]
            m = jnp.maximum(jnp.max(s, axis=-1, keepdims=True), sink)
            p = jnp.exp(s - m)
            denom = jnp.sum(p, axis=-1, keepdims=True) + jnp.exp(sink - m)
            p_ref[j, head] = (p / denom).astype(p_ref.dtype)
        for c in range(SWA_Q // LANES):
            halves = [_dot(p_ref[j, g * SWA_GROUP + c], vcat) for g in range(SWA_KV_HEADS)]
            o_ref[0, rows, c * LANES:(c + 1) * LANES] = jnp.where(low, halves[0], halves[1]).astype(o_ref.dtype)


def _swa_call(qa, ka, va, sinks_l, bias_a, b_sz, s_len):
    L = SWA_BLOCK
    nb = min(SWA_STEP_BLOCKS, s_len // L)
    q3 = qa.reshape(b_sz, s_len, SWA_Q)
    k3 = ka.reshape(b_sz, s_len, SWA_KV)
    v3 = va.reshape(b_sz, s_len, SWA_KV)
    prev = lambda b, i: (b, jnp.maximum(i * nb - 1, 0), 0)
    own = lambda b, i: (b, i, 0)
    out = pl.pallas_call(
        _swa_kernel,
        grid=(b_sz, s_len // (nb * L)),
        in_specs=[
            pl.BlockSpec(memory_space=pltpu.SMEM),
            pl.BlockSpec((1, nb * L, SWA_Q), own),
            pl.BlockSpec((1, L, SWA_KV), prev),
            pl.BlockSpec((1, nb * L, SWA_KV), own),
            pl.BlockSpec((1, L, SWA_KV), prev),
            pl.BlockSpec((1, nb * L, SWA_KV), own),
            pl.BlockSpec((SWA_HEADS, L, 2 * L), lambda b, i: (0, 0, 0)),
        ],
        out_specs=pl.BlockSpec((1, nb * L, SWA_Q), own),
        out_shape=jax.ShapeDtypeStruct((b_sz, s_len, SWA_Q), MXU_DTYPE),
        scratch_shapes=[
            pltpu.VMEM((nb, SWA_HEADS, L, 2 * L), jnp.float32),
            pltpu.VMEM((nb, SWA_HEADS, L, 2 * L), MXU_DTYPE),
        ],
        compiler_params=_params(("arbitrary", "arbitrary")),
        name="swa",
    )(sinks_l, q3, k3, k3, v3, v3, bias_a)
    return out.reshape(b_sz * s_len, SWA_Q)


def _gelu_tanh(x):
    return 0.5 * x * (1.0 + jnp.tanh(math.sqrt(2.0 / math.pi) * (x + 0.044715 * (x * x * x))))


def _rnn_kernel(xr_ref, yr_ref, cw_ref, cb_ref, wg_ref, bg_ref, lam_ref, o_ref, xbuf_ref, h_ref):
    t = pl.program_id(2)
    T = xr_ref.shape[1]
    C = xr_ref.shape[2]
    H = SUBLANES

    @pl.when(t == 0)
    def _():
        xbuf_ref[...] = jnp.zeros(xbuf_ref.shape, jnp.float32)
        h_ref[...] = jnp.zeros(h_ref.shape, jnp.float32)

    x = xr_ref[0]
    halo = xbuf_ref[...]
    row8 = lax.broadcasted_iota(jnp.int32, (H, C), 0)
    xc = x * cw_ref[RNN_CONV - 1:RNN_CONV, :] + cb_ref[...]
    for j in range(1, RNN_CONV):
        xc = xc + _shift_rows(x, j, halo, row8) * cw_ref[RNN_CONV - 1 - j:RNN_CONV - j, :]
    xbuf_ref[...] = x[T - H:T, :]

    gates = _dot(xc.astype(MXU_DTYPE), wg_ref[0]) + bg_ref[0]
    r = _sigmoid(gates[:, :C])
    i = _sigmoid(gates[:, C:])
    nlam = -lam_ref[...]
    softplus = jnp.maximum(nlam, 0.0) + jnp.log1p(jnp.exp(-jnp.abs(nlam)))
    log_a = (-RGLRU_C) * r * softplus
    a = jnp.exp(log_a)
    mult = jnp.sqrt(jnp.tanh(-log_a) * (1.0 + a * a))
    row = lax.broadcasted_iota(jnp.int32, (T, C), 0)
    mult = jnp.where((row == 0) & (t == 0), 1.0, mult)
    u = mult * (i * xc)

    G = T // H
    a = a.reshape(G, H, C)
    u = u.reshape(G, H, C)
    sub = lax.broadcasted_iota(jnp.int32, (G, H, C), 1)
    step = 1
    while step < H:
        keep = sub >= step
        a_prev = jnp.where(keep, pltpu.roll(a, step, 1), 1.0)
        u_prev = jnp.where(keep, pltpu.roll(u, step, 1), 0.0)
        u = a * u_prev + u
        a = a * a_prev
        step *= 2
    carry = h_ref[0:1, :]
    groups = []
    for g in range(G):
        hg = u[g] + a[g] * carry
        groups.append(hg)
        carry = hg[H - 1:H, :]
    h_ref[0:1, :] = carry
    hs = jnp.concatenate(groups, axis=0)
    o_ref[0] = (hs * _gelu_tanh(yr_ref[0])).astype(o_ref.dtype)


def _rnn_call(xr, yr, conv_w, conv_b, wg, bg, lam, b_sz, s_len):
    T = min(T_RNN, s_len)
    C = C_RNN
    nc = RNN_WIDTH // C
    x3 = xr.reshape(b_sz, s_len, RNN_WIDTH)
    y3 = yr.reshape(b_sz, s_len, RNN_WIDTH)
    tile = lambda b, c, t: (b, t, c)
    chan = lambda b, c, t: (0, c)
    out = pl.pallas_call(
        _rnn_kernel,
        grid=(b_sz, nc, s_len // T),
        in_specs=[
            pl.BlockSpec((1, T, C), tile),
            pl.BlockSpec((1, T, C), tile),
            pl.BlockSpec((RNN_CONV, C), chan),
            pl.BlockSpec((1, C), chan),
            pl.BlockSpec((1, C, 2 * C), lambda b, c, t: (c, 0, 0)),
            pl.BlockSpec((1, 1, 2 * C), lambda b, c, t: (c, 0, 0)),
            pl.BlockSpec((1, C), chan),
        ],
        out_specs=pl.BlockSpec((1, T, C), tile),
        out_shape=jax.ShapeDtypeStruct((b_sz, s_len, RNN_WIDTH), MXU_DTYPE),
        scratch_shapes=[
            pltpu.VMEM((SUBLANES, C), jnp.float32),
            pltpu.VMEM((SUBLANES, C), jnp.float32),
        ],
        compiler_params=_params(("arbitrary", "arbitrary", "arbitrary")),
        name="rglru",
    )(x3, y3, conv_w, conv_b, wg, bg, lam)
    return out.reshape(b_sz * s_len, RNN_WIDTH)


def _split_heads(qt):
    rows = lax.broadcasted_iota(jnp.int32, qt.shape, 0)
    zero = jnp.zeros((), qt.dtype)
    return [jnp.where(rows < HEAD_DIM, qt, zero), jnp.where(rows >= HEAD_DIM, qt, zero)]


def _moba_kernel(far_ref, qt_ref, k_ref, vt_ref, kmean_ref, bias_ref, o_ref, sel_ref, qs_ref, s0_ref, s1_ref,
                 s2_ref, s3_ref, mx_ref, pva_ref, pvb_ref, ala_ref, alb_ref, m_ref, acc_ref):
    n = pl.program_id(2)
    MB = MOBA_BLOCK
    nb = kmean_ref.shape[1]
    qts = _split_heads(qt_ref[0, 0])

    kmean = kmean_ref[0].astype(MXU_DTYPE)
    blk = lax.broadcasted_iota(jnp.int32, (nb, MB), 0).astype(jnp.float32)
    n_f = n.astype(jnp.float32)
    past = blk < n_f
    for e in range(2):
        gate = _dot(kmean, qts[e])
        gate = jnp.where(past, gate, NEG_INF)
        sel = jnp.zeros((nb, MB), jnp.float32)
        for _ in range(MOBA_TOPK):
            mx = jnp.max(gate, axis=0, keepdims=True)
            first = jnp.min(jnp.where(gate == mx, blk, float(nb)), axis=0, keepdims=True)
            hit = blk == first
            sel = jnp.where(hit, 1.0, sel)
            gate = jnp.where(hit, TAKEN, gate)
        sel_ref[e] = jnp.where(blk == n_f, 1.0, jnp.where(past, sel, 0.0))

    for e in range(2):
        qs_ref[e] = qts[e]
    m_ref[...] = jnp.full(m_ref.shape, NEG_INF, jnp.float32)
    acc_ref[...] = jnp.zeros(acc_ref.shape, jnp.float32)
    pvb_ref[...] = jnp.zeros(pvb_ref.shape, jnp.float32)
    alb_ref[...] = jnp.ones(alb_ref.shape, jnp.float32)
    ones_rows = jnp.ones((MOBA_SUM_ROWS, MB), MXU_DTYPE)

    def block_of(t):
        return jnp.clip(n - t, 0, n)

    def accumulate(pv_ref, al_ref):
        for e in range(2):
            acc_ref[e] = al_ref[e] * acc_ref[e] + pv_ref[e]

    far_bias = [far_ref[2 * pl.program_id(1) + e] for e in range(2)]

    s_bufs = (s0_ref, s1_ref, s2_ref, s3_ref)

    def scores(t, slot, far):
        m = block_of(t)
        k_m = k_ref[0, pl.ds(pl.multiple_of(m * MB, MB), MB), :]
        for e in range(2):
            dims = slice(e * HEAD_DIM, (e + 1) * HEAD_DIM)
            s = _dot(k_m[:, dims], qs_ref[e, dims, :])
            if not far:
                s = s + bias_ref[e, t]
            s_bufs[slot][e] = s
            part = s[0:SUBLANES, :]
            for g in range(1, MB // SUBLANES):
                part = jnp.maximum(part, s[g * SUBLANES:(g + 1) * SUBLANES, :])
            mx_ref[slot, e] = part

    def softmax(t, slot, pv_ref, al_ref, far):
        m = block_of(t)
        live = t <= n
        vt_m = vt_ref[0, m]
        for e in range(2):
            s = s_bufs[slot][e]
            chosen = (sel_ref[e, pl.ds(m, 1), :] > 0.0) & live
            mx = jnp.max(mx_ref[slot, e], axis=0, keepdims=True)
            if far:
                mx = mx + far_bias[e]
            mx = jnp.where(chosen, mx, NEG_INF)
            m_old = m_ref[e]
            m_new = jnp.maximum(m_old, mx)
            al_ref[e] = jnp.exp(m_old - m_new)
            sub = m_new - far_bias[e] if far else m_new
            p = jnp.exp(s - jnp.where(chosen, sub, -NEG_INF)).astype(MXU_DTYPE)
            m_ref[e] = m_new
            lhs = jnp.concatenate([vt_m[e * HEAD_DIM:(e + 1) * HEAD_DIM, :], ones_rows], axis=0)
            pv_ref[e] = _dot(lhs, p)

    pv_bufs = ((pva_ref, ala_ref), (pvb_ref, alb_ref))

    def region(t, slot, far, far_ahead):
        pv_cur, al_cur = pv_bufs[slot % 2]
        pv_oth, al_oth = pv_bufs[1 - slot % 2]
        scores(t + MOBA_AHEAD, (slot + MOBA_AHEAD) % MOBA_UNROLL, far_ahead)
        accumulate(pv_oth, al_oth)
        softmax(t, slot, pv_cur, al_cur, far)

    def near_regions(t_lo, t_hi):
        for t in range(t_lo, t_hi):
            region(t, t % MOBA_UNROLL, t >= MOBA_NEAR, t + MOBA_AHEAD >= MOBA_NEAR)

    near_steps = 2 * MOBA_UNROLL
    for t in range(MOBA_AHEAD):
        scores(t, t, False)
    near_regions(0, MOBA_UNROLL)

    @pl.when(n >= MOBA_UNROLL)
    def _():
        near_regions(MOBA_UNROLL, near_steps)

    def far_regions(t0, slot0, count):
        for r in range(count):
            region(t0 + r, slot0 + r, True, True)

    n_far = jnp.maximum(n + 1 - near_steps, 0)
    n_quads = (n_far + 1) // MOBA_UNROLL
    n_pairs = jnp.where(n_far > MOBA_UNROLL * n_quads, 1, 0)

    def quad_body(i, carry):
        far_regions(near_steps + MOBA_UNROLL * i, 0, MOBA_UNROLL)
        return carry

    lax.fori_loop(0, n_quads, quad_body, 0)
    t_pairs = near_steps + MOBA_UNROLL * n_quads

    @pl.when(n_pairs > 0)
    def _():
        far_regions(t_pairs, 0, 2)

    accumulate(pvb_ref, alb_ref)

    top = acc_ref[0, 0:HEAD_DIM, :] / acc_ref[0, HEAD_DIM:HEAD_DIM + 1, :]
    bot = acc_ref[1, 0:HEAD_DIM, :] / acc_ref[1, HEAD_DIM:HEAD_DIM + 1, :]
    o_ref[0] = jnp.concatenate([top, bot], axis=0).T.astype(o_ref.dtype)


def _moba_call(qct, kc, vct, kmean, bias_c, far_c, b_sz, s_len):
    MB = MOBA_BLOCK
    nb = s_len // MB
    k3 = kc.reshape(b_sz, s_len, MOBA_W)
    km3 = kmean.reshape(b_sz, nb, MOBA_W)
    npair = MOBA_W // LANES
    out = pl.pallas_call(
        _moba_kernel,
        grid=(b_sz, npair, nb),
        in_specs=[
            pl.BlockSpec(memory_space=pltpu.SMEM),
            pl.BlockSpec((1, 1, LANES, MB), lambda b, p, n: (b, n, p, 0)),
            pl.BlockSpec((1, s_len, LANES), lambda b, p, n: (b, 0, p)),
            pl.BlockSpec((1, nb, LANES, MB), lambda b, p, n: (b, 0, p, 0)),
            pl.BlockSpec((1, nb, LANES), lambda b, p, n: (b, 0, p)),
            pl.BlockSpec((2, MOBA_NEAR, MB, MB), lambda b, p, n: (p, 0, 0, 0)),
        ],
        out_specs=pl.BlockSpec((1, MB, LANES), lambda b, p, n: (b, n, p)),
        out_shape=jax.ShapeDtypeStruct((b_sz, s_len, MOBA_W), MXU_DTYPE),
        scratch_shapes=[
            pltpu.VMEM((2, nb, MB), jnp.float32),
            pltpu.VMEM((2, LANES, MB), MXU_DTYPE),
            pltpu.VMEM((2, MB, MB), jnp.float32),
            pltpu.VMEM((2, MB, MB), jnp.float32),
            pltpu.VMEM((2, MB, MB), jnp.float32),
            pltpu.VMEM((2, MB, MB), jnp.float32),
            pltpu.VMEM((MOBA_UNROLL, 2, SUBLANES, MB), jnp.float32),
            pltpu.VMEM((2, HEAD_DIM + MOBA_SUM_ROWS, MB), jnp.float32),
            pltpu.VMEM((2, HEAD_DIM + MOBA_SUM_ROWS, MB), jnp.float32),
            pltpu.VMEM((2, 1, MB), jnp.float32),
            pltpu.VMEM((2, 1, MB), jnp.float32),
            pltpu.VMEM((2, 1, MB), jnp.float32),
            pltpu.VMEM((2, HEAD_DIM + MOBA_SUM_ROWS, MB), jnp.float32),
        ],
        compiler_params=_params(("arbitrary", "arbitrary", "arbitrary")),
        name="moba",
    )(far_c, qct, k3, vct, km3, bias_c)
    return out.reshape(b_sz * s_len, MOBA_W)


def _merge_kernel(x_ref, oa_ref, ob_ref, oc_ref, g_ref, mod_ref, wa_ref, wb_ref, wc_ref, wo_ref, o_ref):
    d = D_MODEL
    merged = _sigmoid(g_ref[:, 0:d]) * _dot(oa_ref[...], wa_ref[...])
    merged = merged + _sigmoid(g_ref[:, d:2 * d]) * _dot(ob_ref[...], wb_ref[...])
    merged = merged + _sigmoid(g_ref[:, 2 * d:3 * d]) * _dot(oc_ref[...], wc_ref[...])
    y = _dot(merged.astype(MXU_DTYPE), wo_ref[...])
    gate_m = mod_ref[0, :, 2 * d:3 * d]
    o_ref[...] = x_ref[...] + gate_m * y


def _merge_call(x2, o_a, o_b, o_c, g_logits, mods_l, wa, wb, wc, wo, s_len):
    n, d = x2.shape
    tm = min(TM_MERGE, s_len)
    tpb = s_len // tm
    row = lambda i: (i, 0)
    const2 = lambda i: (0, 0)
    return pl.pallas_call(
        _merge_kernel,
        grid=(n // tm,),
        in_specs=[
            pl.BlockSpec((tm, d), row),
            pl.BlockSpec((tm, SWA_Q), row),
            pl.BlockSpec((tm, RNN_WIDTH), row),
            pl.BlockSpec((tm, MOBA_W), row),
            pl.BlockSpec((tm, 3 * d), row),
            pl.BlockSpec((1, 1, 6 * d), lambda i: (i // tpb, 0, 0)),
            pl.BlockSpec(wa.shape, const2),
            pl.BlockSpec(wb.shape, const2),
            pl.BlockSpec(wc.shape, const2),
            pl.BlockSpec(wo.shape, const2),
        ],
        out_specs=pl.BlockSpec((tm, d), row),
        out_shape=jax.ShapeDtypeStruct((n, d), jnp.float32),
        compiler_params=_params(("arbitrary",)),
        name="merge",
    )(x2, o_a, o_b, o_c, g_logits, mods_l, wa, wb, wc, wo)


def _ffn_kernel(x_ref, mod_ref, gain_ref, wu_ref, cw_ref, cb_ref, wd_ref, o_ref, h_ref, act_ref, halo_ref,
                *, tiles_per_seq):
    i = pl.program_id(0)
    d = D_MODEL
    tm = x_ref.shape[0]
    H = SUBLANES
    CH = FFN_CHUNK

    @pl.when(i == 0)
    def _():
        halo_ref[...] = jnp.zeros(halo_ref.shape, jnp.float32)

    x = x_ref[...]
    shift = mod_ref[0, :, 3 * d:4 * d]
    scale = mod_ref[0, :, 4 * d:5 * d]
    ms = jnp.mean(x * x, axis=-1, keepdims=True)
    y = x * lax.rsqrt(ms + RMS_EPS) * gain_ref[...]
    h_ref[...] = (y * (1.0 + scale) + shift).astype(h_ref.dtype)

    seq_start = (i % tiles_per_seq) == 0
    row8 = lax.broadcasted_iota(jnp.int32, (H, CH), 0)

    def conv(col):
        cols = slice(col, col + CH)
        u = _dot(h_ref[...], wu_ref[:, cols])
        halo = jnp.where(seq_start, 0.0, halo_ref[:, cols])
        halo_ref[:, cols] = u[tm - H:tm, :]
        out = u * cw_ref[FFN_CONV - 1:FFN_CONV, cols] + cb_ref[:, cols]
        for k in range(1, FFN_CONV):
            out = out + _shift_rows(u, k, halo, row8) * cw_ref[FFN_CONV - 1 - k:FFN_CONV - k, cols]
        return out

    for c in range(D_FF // CH):
        ug = conv(c * CH)
        uv = conv(D_FF + c * CH)
        act_ref[:, c * CH:(c + 1) * CH] = (ug * _sigmoid(ug) * uv).astype(act_ref.dtype)

    gate_f = mod_ref[0, :, 5 * d:6 * d]
    o_ref[...] = x_ref[...] + gate_f * _dot(act_ref[...], wd_ref[...])


def _ffn_call(x2, mods_l, gain, w_up, conv_w, conv_b, w_down, s_len):
    n, d = x2.shape
    tm = min(TM_FFN, s_len)
    tpb = s_len // tm
    row = lambda i: (i, 0)
    const2 = lambda i: (0, 0)
    kern = functools.partial(_ffn_kernel, tiles_per_seq=tpb)
    return pl.pallas_call(
        kern,
        grid=(n // tm,),
        in_specs=[
            pl.BlockSpec((tm, d), row),
            pl.BlockSpec((1, 1, 6 * d), lambda i: (i // tpb, 0, 0)),
            pl.BlockSpec((1, d), const2),
            pl.BlockSpec(w_up.shape, const2, pipeline_mode=pl.Buffered(1)),
            pl.BlockSpec(conv_w.shape, const2),
            pl.BlockSpec(conv_b.shape, const2),
            pl.BlockSpec(w_down.shape, const2, pipeline_mode=pl.Buffered(1)),
        ],
        out_specs=pl.BlockSpec((tm, d), row),
        out_shape=jax.ShapeDtypeStruct((n, d), jnp.float32),
        scratch_shapes=[
            pltpu.VMEM((tm, d), MXU_DTYPE),
            pltpu.VMEM((tm, D_FF), MXU_DTYPE),
            pltpu.VMEM((SUBLANES, 2 * D_FF), jnp.float32),
        ],
        compiler_params=_params(("arbitrary",)),
        name="ffn",
    )(x2, mods_l, gain, w_up, conv_w, conv_b, w_down)


def _swa_head_order():
    order = []
    for c in range(SWA_Q // LANES):
        for g in range(SWA_KV_HEADS):
            order.append(g * SWA_GROUP + c)
    return order


def _bias_tables(rel_bias):
    L = SWA_BLOCK
    qi = np.arange(L)[:, None]
    sj = np.arange(2 * L)[None, :]
    bucket_a = _t5_bucket_np(qi + L - sj)[None]
    tab_a = _bias_tiles_call(rel_bias, bucket_a, 0, SWA_HEADS)[:, 0]
    MB = MOBA_BLOCK
    kj = np.arange(MB)[:, None]
    qq = np.arange(MB)[None, :]
    deltas = np.arange(MOBA_NEAR)[:, None, None]
    bucket_c = _t5_bucket_np(deltas * MB + qq[None] - kj[None])
    bucket_c = np.where((deltas > 0) | (kj <= qq)[None], bucket_c, NUM_BUCKETS)
    tab_c = _bias_tiles_call(rel_bias, bucket_c, SWA_HEADS, MOBA_HEADS)
    far_c = rel_bias[NUM_BUCKETS - 1, SWA_HEADS:]
    return tab_a, tab_c, far_c


def kernel(x, c, w_mod, b_mod, norm_mix, norm_ffn, w_in, qnorm_a, knorm_a, sinks, rnn_conv_w, rnn_conv_b,
           rnn_gate_a_w, rnn_gate_a_b, rnn_gate_x_w, rnn_gate_x_b, rnn_lambda, qnorm_c, knorm_c, rel_bias,
           w_branch, w_out, w_up, ffn_conv_w, ffn_conv_b, w_down):
    b_sz, s_len, d = x.shape
    depth = w_in.shape[0]
    assert d == D_MODEL and s_len % MOBA_BLOCK == 0 and b_sz <= SUBLANES
    n = b_sz * s_len
    bf = MXU_DTYPE
    f32 = jnp.float32

    c_pad = jnp.zeros((SUBLANES, d), f32).at[:b_sz].set(c)
    mods = _mod_call(c_pad, w_mod, b_mod)[:, :b_sz]
    mods = mods.reshape(depth, b_sz, 1, 6 * d)

    offs = np.cumsum([0, SWA_Q, SWA_KV, SWA_KV, RNN_WIDTH, RNN_WIDTH, MOBA_W, MOBA_W, MOBA_W, 3 * d])
    o_qa, o_ka, o_va, o_xr, o_yr, o_qc, o_kc, o_vc, o_g, o_end = [int(v) for v in offs]
    head_order = _swa_head_order()
    w_in_b = w_in.astype(bf)
    wn_all = jnp.concatenate(
        [w_in_b[:, :, o_qa + h * HEAD_DIM:o_qa + (h + 1) * HEAD_DIM] for h in head_order]
        + [w_in_b[:, :, o_ka:o_qc], w_in_b[:, :, o_kc:o_vc], w_in_b[:, :, o_g:o_end]], axis=-1)
    wt_all = jnp.swapaxes(jnp.concatenate([w_in_b[:, :, o_qc:o_kc], w_in_b[:, :, o_vc:o_g]], axis=-1), 1, 2)
    lane_head = np.arange(LANES) // HEAD_DIM
    ones_blk = jnp.asarray(lane_head[:, None] == lane_head[None, :], bf)

    wa_all = jnp.concatenate(
        [w_branch[:, h * HEAD_DIM:(h + 1) * HEAD_DIM] for h in head_order], axis=1).astype(bf)
    wb_all = w_branch[:, SWA_Q:SWA_Q + RNN_WIDTH].astype(bf)
    wc_all = w_branch[:, SWA_Q + RNN_WIDTH:].astype(bf)
    wo_all = w_out.astype(bf)
    wup_all = w_up.astype(bf)
    wdn_all = w_down.astype(bf)

    per = C_RNN // RNN_BLOCK_WIDTH
    nct = RNN_WIDTH // C_RNN
    eye = jnp.eye(per, dtype=f32)

    def blockdiag(w):
        w5 = w.reshape(depth, nct, per, RNN_BLOCK_WIDTH, RNN_BLOCK_WIDTH)
        full = jnp.einsum('lcpij,pq->lcpiqj', w5, eye)
        return full.reshape(depth, nct, C_RNN, C_RNN)

    wg_all = jnp.concatenate([blockdiag(rnn_gate_a_w), blockdiag(rnn_gate_x_w)], axis=-1).astype(bf)
    bg_all = jnp.concatenate([rnn_gate_a_b.reshape(depth, nct, 1, C_RNN),
                              rnn_gate_x_b.reshape(depth, nct, 1, C_RNN)], axis=-1)

    tab_a, tab_c, far_c = _bias_tables(rel_bias)

    x2 = x.reshape(n, d)
    for l in range(depth):
        gq_a = jnp.tile(qnorm_a[l], SWA_Q // HEAD_DIM)[None, :]
        gk_a = jnp.tile(knorm_a[l], SWA_KV // HEAD_DIM)[None, :]
        gk_c = jnp.tile(knorm_c[l], MOBA_W // HEAD_DIM)[None, :]
        gq_c = jnp.tile(qnorm_c[l], MOBA_W // HEAD_DIM)[:, None]
        qa, ka, va, xr, yr, kc, kmean, g_logits, qct, vct = _inproj_call(
            x2, mods[l], norm_mix[l][None, :], wn_all[l], wt_all[l], ones_blk, gq_a, gk_a, gk_c, gq_c,
            b_sz, s_len)
        o_a = _swa_call(qa, ka, va, sinks[l], tab_a, b_sz, s_len)
        o_b = _rnn_call(xr, yr, rnn_conv_w[l], rnn_conv_b[l][None, :], wg_all[l], bg_all[l],
                        rnn_lambda[l][None, :], b_sz, s_len)
        o_c = _moba_call(qct, kc, vct, kmean, tab_c, far_c, b_sz, s_len)
        x2 = _merge_call(x2, o_a, o_b, o_c, g_logits, mods[l], wa_all[l], wb_all[l], wc_all[l], wo_all[l],
                         s_len)
        x2 = _ffn_call(x2, mods[l], norm_ffn[l][None, :], wup_all[l], ffn_conv_w[l], ffn_conv_b[l][None, :],
                       wdn_all[l], s_len)
    return x2.reshape(b_sz, s_len, d)
```
